```python
import jax
import jax.numpy as jnp
from jax import lax
import numpy as np

D_MODEL = 1024
BATCH = 2
SEQ = 8192
DEPTH = 4
DEC_BATCH = 128
DEC_SEQ = 4
PAST_LEN = 8192
PAGE_SIZE = 128

N_MIXERS = 3
HEAD_DIM = 64
N_HEADS = D_MODEL // HEAD_DIM
N_KV_HEADS = 4
GROUP = N_HEADS // N_KV_HEADS
ATTN_SCALE = HEAD_DIM ** -0.5
Q_BLOCK = 128
WINDOW = 128
RET_HEADS = 4
RET_DK = D_MODEL // RET_HEADS
RET_DV = 2 * D_MODEL // RET_HEADS
RET_CHUNK = 128
RET_THETA_BASE = 10000.0
D_FF = 4 * D_MODEL
D_PLE = 256
EPS = 1e-6
FORGET_BIAS = 6.0
N_FOX = (DEPTH + N_MIXERS - 1) // N_MIXERS
N_SWA = (DEPTH + N_MIXERS - 2) // N_MIXERS
N_RET = DEPTH // N_MIXERS

kernel_name = 'hybrid_fox_swa_retention_decode_step'


def rms_norm(x, g):
    xf = x.astype(jnp.float32)
    y = xf * lax.rsqrt(jnp.mean(xf * xf, axis=-1, keepdims=True) + EPS)
    return (y * g.astype(jnp.float32)).astype(x.dtype)


def sq_relu_mlp(x, w_up, w_down):
    h = jax.nn.relu(x @ w_up)
    return (h * h) @ w_down


def ple_term(h, p_i, g_norm, w_gate, w_proj):
    gate = jax.nn.sigmoid(rms_norm(h, g_norm) @ w_gate)
    return gate * (p_i @ w_proj)


def sink_softmax(s, sink):
    col = jnp.broadcast_to(sink.astype(jnp.float32).reshape(N_KV_HEADS, GROUP, 1, 1), s.shape[:-1] + (1,))
    p = jax.nn.softmax(jnp.concatenate([s, col], axis=-1), axis=-1)
    return p[..., :-1]


def fox_project(xn, wq, wk, wv, wf, bf):
    B, T, _ = xn.shape
    q = (xn @ wq).reshape(B, T, N_KV_HEADS, GROUP, HEAD_DIM)
    k = (xn @ wk).reshape(B, T, N_KV_HEADS, HEAD_DIM)
    v = (xn @ wv).reshape(B, T, N_KV_HEADS, HEAD_DIM)
    logf = jax.nn.log_sigmoid((xn @ wf + bf).astype(jnp.float32))
    return q, k, v, logf


def fox_prompt(q, k, v, logf):
    B, T = q.shape[:2]
    nb = T // Q_BLOCK
    c = jnp.cumsum(logf, axis=1).reshape(B, T, N_KV_HEADS, GROUP)
    c_keys = jnp.transpose(c, (0, 2, 3, 1))[:, :, :, None, :]
    qb = jnp.swapaxes(q.reshape(B, nb, Q_BLOCK, N_KV_HEADS, GROUP, HEAD_DIM), 0, 1)
    cb = jnp.swapaxes(c.reshape(B, nb, Q_BLOCK, N_KV_HEADS, GROUP), 0, 1)
    kpos = jnp.arange(T)

    def block(args):
        q_i, c_i, start = args
        s = jnp.einsum('bqkgd,bskd->bkgqs', q_i, k).astype(jnp.float32) * ATTN_SCALE
        s = s + jnp.transpose(c_i, (0, 2, 3, 1))[..., None] - c_keys
        qpos = start + jnp.arange(Q_BLOCK)
        s = jnp.where(kpos[None, :] <= qpos[:, None], s, -jnp.inf)
        p = jax.nn.softmax(s, axis=-1).astype(v.dtype)
        return jnp.einsum('bkgqs,bskd->bqkgd', p, v)

    o = lax.map(block, (qb, cb, jnp.arange(nb) * Q_BLOCK))
    return jnp.swapaxes(o, 0, 1).reshape(B, T, N_HEADS * HEAD_DIM)


def fox_sample(q, k_new, v_new, lf_new, k_past, v_past, lf_past):
    Bd, T = q.shape[:2]
    P = k_past.shape[1]
    lf_past = lf_past.astype(jnp.float32)
    suffix = lax.cumsum(lf_past, axis=1, reverse=True) - lf_past
    cn = jnp.cumsum(lf_new, axis=1)
    cn_g = jnp.transpose(cn.reshape(Bd, T, N_KV_HEADS, GROUP), (0, 2, 3, 1))
    suf_g = jnp.transpose(suffix.reshape(Bd, P, N_KV_HEADS, GROUP), (0, 2, 3, 1))
    s_past = jnp.einsum('btkgd,bskd->bkgts', q, k_past.astype(q.dtype)).astype(jnp.float32) * ATTN_SCALE
    s_past = s_past + cn_g[..., None] + suf_g[:, :, :, None, :]
    s_new = jnp.einsum('btkgd,bskd->bkgts', q, k_new).astype(jnp.float32) * ATTN_SCALE
    s_new = s_new + cn_g[..., None] - cn_g[:, :, :, None, :]
    tq = jnp.arange(T)
    s_new = jnp.where(tq[None, :] <= tq[:, None], s_new, -jnp.inf)
    p = jax.nn.softmax(jnp.concatenate([s_past, s_new], axis=-1), axis=-1).astype(v_new.dtype)
    o = (jnp.einsum('bkgts,bskd->btkgd', p[..., :P], v_past.astype(v_new.dtype))
         + jnp.einsum('bkgts,bskd->btkgd', p[..., P:], v_new))
    return o.reshape(Bd, T, N_HEADS * HEAD_DIM)


def gqa_project(xn, wq, wk, wv):
    B, T, _ = xn.shape
    q = (xn @ wq).reshape(B, T, N_KV_HEADS, GROUP, HEAD_DIM)
    k = (xn @ wk).reshape(B, T, N_KV_HEADS, HEAD_DIM)
    v = (xn @ wv).reshape(B, T, N_KV_HEADS, HEAD_DIM)
    return q, k, v


def swa_prompt(q, k, v, sink):
    B, T = q.shape[:2]
    nb = T // WINDOW
    qb = q.reshape(B, nb, WINDOW, N_KV_HEADS, GROUP, HEAD_DIM)

    def with_prev(x):
        xb = x.reshape(B, nb, WINDOW, N_KV_HEADS, HEAD_DIM)
        prev = jnp.pad(xb, ((0, 0), (1, 0), (0, 0), (0, 0), (0, 0)))[:, :-1]
        return jnp.concatenate([prev, xb], axis=2)

    kk, vv = with_prev(k), with_prev(v)
    s = jnp.einsum('bnqkgd,bnskd->bnkgqs', qb, kk).astype(jnp.float32) * ATTN_SCALE
    blk = jnp.arange(nb)[:, None] * WINDOW
    qpos = blk + jnp.arange(WINDOW)[None, :]
    kpos = blk - WINDOW + jnp.arange(2 * WINDOW)[None, :]
    rel = qpos[:, :, None] - kpos[:, None, :]
    mask = (rel >= 0) & (rel < WINDOW) & (kpos[:, None, :] >= 0)
    s = jnp.where(mask[None, :, None, None], s, -jnp.inf)
    p = sink_softmax(s, sink).astype(v.dtype)
    o = jnp.einsum('bnkgqs,bnskd->bnqkgd', p, vv)
    return o.reshape(B, T, N_HEADS * HEAD_DIM)


def swa_sample(q, k_new, v_new, k_buf, v_buf, sink):
    Bd, T = q.shape[:2]
    L = k_buf.shape[1]
    kk = jnp.concatenate([k_buf.astype(k_new.dtype), k_new], axis=1)
    vv = jnp.concatenate([v_buf.astype(v_new.dtype), v_new], axis=1)
    s = jnp.einsum('btkgd,bskd->bkgts', q, kk).astype(jnp.float32) * ATTN_SCALE
    rel = jnp.arange(T)[:, None] - (jnp.arange(L + T)[None, :] - L)
    s = jnp.where((rel >= 0) & (rel < WINDOW), s, -jnp.inf)
    p = sink_softmax(s, sink).astype(vv.dtype)
    o = jnp.einsum('bkgts,bskd->btkgd', p, vv).reshape(Bd, T, N_HEADS * HEAD_DIM)
    return o, kk[:, -L:], vv[:, -L:]


def ret_log_decay():
    return jnp.log(1.0 - 2.0 ** (-5.0 - jnp.arange(RET_HEADS, dtype=jnp.float32)))


def theta_shift(x, pos):
    freqs = 1.0 / (RET_THETA_BASE ** jnp.linspace(0.0, 1.0, RET_DK // 2, dtype=jnp.float32))
    ang = pos.astype(jnp.float32)[:, None] * freqs[None, :]
    cos = jnp.cos(ang)[None, :, None, :]
    sin = jnp.sin(ang)[None, :, None, :]
    x1, x2 = x[..., 0::2], x[..., 1::2]
    return jnp.stack([x1 * cos - x2 * sin, x1 * sin + x2 * cos], axis=-1).reshape(x.shape)


def ret_project(xn, wq, wk, wv, pos):
    B, T, _ = xn.shape
    q = (xn @ wq).reshape(B, T, RET_HEADS, RET_DK).astype(jnp.float32)
    k = (xn @ wk).reshape(B, T, RET_HEADS, RET_DK).astype(jnp.float32) * (RET_DK ** -0.5)
    v = (xn @ wv).reshape(B, T, RET_HEADS, RET_DV).astype(jnp.float32)
    return theta_shift(q, pos), theta_shift(k, pos), v


def retention_chunk(R, q, k, v):
    L = q.shape[1]
    lg = ret_log_decay()
    idx = jnp.arange(L, dtype=jnp.float32)
    rel = idx[:, None] - idx[None, :]
    decay = jnp.where(rel >= 0, jnp.exp(lg[:, None, None] * jnp.maximum(rel, 0.0)), 0.0)
    s = jnp.einsum('bqhd,bshd->bhqs', q, k) * decay
    intra = jnp.einsum('bhqs,bshe->bqhe', s, v)
    xi = jnp.exp(lg[None, :] * (idx[:, None] + 1.0))
    cross = jnp.einsum('bqhd,bhde->bqhe', q * xi[None, :, :, None], R)
    zeta = jnp.exp(lg[None, :] * (L - 1.0 - idx[:, None]))
    R_new = (R * jnp.exp(lg * L)[None, :, None, None]
             + jnp.einsum('bshd,bshe->bhde', k * zeta[None, :, :, None], v))
    return R_new, intra + cross


def retention_prompt(q, k, v):
    B, T = q.shape[:2]
    nc = T // RET_CHUNK

    def to_chunks(x):
        return jnp.swapaxes(x.reshape((B, nc, RET_CHUNK) + x.shape[2:]), 0, 1)

    R0 = jnp.zeros((B, RET_HEADS, RET_DK, RET_DV), jnp.float32)
    R, o = lax.scan(lambda R, xs: retention_chunk(R, *xs), R0, (to_chunks(q), to_chunks(k), to_chunks(v)))
    return R, jnp.swapaxes(o, 0, 1).reshape(B, T, RET_HEADS, RET_DV)


def ret_output(o, xn, wg, gn, wo):
    B, T = o.shape[:2]
    mu = jnp.mean(o, axis=-1, keepdims=True)
    var = jnp.mean(jnp.square(o - mu), axis=-1, keepdims=True)
    y = ((o - mu) * lax.rsqrt(var + EPS)).reshape(B, T, RET_HEADS * RET_DV) * gn.astype(jnp.float32)
    return (jax.nn.silu(xn @ wg) * y.astype(xn.dtype)) @ wo


def setup_inputs(seed: int = 0) -> dict:
    key = jax.random.key(seed)
    keys = iter(jax.random.split(key, 64))

    def normal(shape, scale=1.0):
        return jax.random.normal(next(keys), shape, jnp.float32) * scale

    def gain(shape):
        return 1.0 + 0.1 * normal(shape)

    D = D_MODEL
    HQ = N_HEADS * HEAD_DIM
    HKV = N_KV_HEADS * HEAD_DIM
    RQ = RET_HEADS * RET_DK
    RV = RET_HEADS * RET_DV
    n_pages = PAST_LEN // PAGE_SIZE
    n_used = DEC_BATCH * n_pages
    n_pool = n_used + n_used // 4
    win = min(WINDOW, PAST_LEN)
    perm = jax.random.permutation(next(keys), n_pool)
    page_table = perm[:n_used].reshape(DEC_BATCH, n_pages).astype(jnp.int32)
    return {
        'x_prompt': normal((BATCH, SEQ, D)),
        'x_sample': normal((DEC_BATCH, DEC_SEQ, D)),
        'cache_fox_k': normal((N_FOX, n_pool, PAGE_SIZE, N_KV_HEADS, HEAD_DIM)),
        'cache_fox_v': normal((N_FOX, n_pool, PAGE_SIZE, N_KV_HEADS, HEAD_DIM)),
        'cache_fox_logf': jax.nn.log_sigmoid(FORGET_BIAS + 0.5 * normal((N_FOX, n_pool, PAGE_SIZE, N_HEADS))),
        'cache_swa_k': normal((N_SWA, DEC_BATCH, win, N_KV_HEADS, HEAD_DIM)),
        'cache_swa_v': normal((N_SWA, DEC_BATCH, win, N_KV_HEADS, HEAD_DIM)),
        'state_ret': normal((N_RET, DEC_BATCH, RET_HEADS, RET_DK, RET_DV), 0.25),
        'page_table': page_table,
        'p_prompt': normal((DEPTH, BATCH, SEQ, D_PLE)),
        'p_sample': normal((DEPTH, DEC_BATCH, DEC_SEQ, D_PLE)),
        'fox_wq': normal((N_FOX, D, HQ), D ** -0.5),
        'fox_wk': normal((N_FOX, D, HKV), D ** -0.5),
        'fox_wv': normal((N_FOX, D, HKV), D ** -0.5),
        'fox_wf': normal((N_FOX, D, N_HEADS), D ** -0.5),
        'fox_bf': FORGET_BIAS + 0.5 * normal((N_FOX, N_HEADS)),
        'fox_wo': normal((N_FOX, HQ, D), HQ ** -0.5),
        'swa_wq': normal((N_SWA, D, HQ), D ** -0.5),
        'swa_wk': normal((N_SWA, D, HKV), D ** -0.5),
        'swa_wv': normal((N_SWA, D, HKV), D ** -0.5),
        'swa_sink': normal((N_SWA, N_HEADS)),
        'swa_wo': normal((N_SWA, HQ, D), HQ ** -0.5),
        'ret_wq': normal((N_RET, D, RQ), D ** -0.5),
        'ret_wk': normal((N_RET, D, RQ), D ** -0.5),
        'ret_wv': normal((N_RET, D, RV), D ** -0.5),
        'ret_wg': normal((N_RET, D, RV), D ** -0.5),
        'ret_gn': gain((N_RET, RV)),
        'ret_wo': normal((N_RET, RV, D), RV ** -0.5),
        'norm_mix': gain((DEPTH, D)),
        'norm_mlp': gain((DEPTH, D)),
        'mlp_up': normal((DEPTH, D, D_FF), D ** -0.5),
        'mlp_down': normal((DEPTH, D_FF, D), D_FF ** -0.5),
        'norm_ple': gain((DEPTH, D)),
        'ple_gate': normal((DEPTH, D, D), D ** -0.5),
        'ple_proj': normal((DEPTH, D_PLE, D), D_PLE ** -0.5),
        'norm_final': gain((D,)),
    }


def reference(x_prompt, x_sample, cache_fox_k, cache_fox_v, cache_fox_logf, cache_swa_k, cache_swa_v,
              state_ret, page_table, p_prompt, p_sample,
              fox_wq, fox_wk, fox_wv, fox_wf, fox_bf, fox_wo,
              swa_wq, swa_wk, swa_wv, swa_sink, swa_wo,
              ret_wq, ret_wk, ret_wv, ret_wg, ret_gn, ret_wo,
              norm_mix, norm_mlp, mlp_up, mlp_down, norm_ple, ple_gate, ple_proj, norm_final):
    Bd, Ts = x_sample.shape[:2]
    Tp = x_prompt.shape[1]
    P = page_table.shape[1] * PAGE_SIZE
    pos_p = jnp.arange(Tp)
    pos_s = P + jnp.arange(Ts)
    hp, hs = x_prompt, x_sample
    fkp, fvp, flp, fks, fvs, fls = [], [], [], [], [], []
    skp, svp, sks, svs = [], [], [], []
    rsp, rss = [], []

    for i in range(DEPTH):
        kind = i % N_MIXERS
        j = i // N_MIXERS
        xp = rms_norm(hp, norm_mix[i])
        xs = rms_norm(hs, norm_mix[i])
        if kind == 0:
            qp, kp, vp, lfp = fox_project(xp, fox_wq[j], fox_wk[j], fox_wv[j], fox_wf[j], fox_bf[j])
            qs, ks, vs, lfs = fox_project(xs, fox_wq[j], fox_wk[j], fox_wv[j], fox_wf[j], fox_bf[j])
            k_past = cache_fox_k[j][page_table].reshape(Bd, P, N_KV_HEADS, HEAD_DIM)
            v_past = cache_fox_v[j][page_table].reshape(Bd, P, N_KV_HEADS, HEAD_DIM)
            lf_past = cache_fox_logf[j][page_table].reshape(Bd, P, N_HEADS)
            mp = fox_prompt(qp, kp, vp, lfp) @ fox_wo[j]
            ms = fox_sample(qs, ks, vs, lfs, k_past, v_past, lf_past) @ fox_wo[j]
            fkp.append(kp)
            fvp.append(vp)
            flp.append(lfp)
            fks.append(ks)
            fvs.append(vs)
            fls.append(lfs)
        elif kind == 1:
            qp, kp, vp = gqa_project(xp, swa_wq[j], swa_wk[j], swa_wv[j])
            qs, ks, vs = gqa_project(xs, swa_wq[j], swa_wk[j], swa_wv[j])
            mp = swa_prompt(qp, kp, vp, swa_sink[j]) @ swa_wo[j]
            os_, kbuf, vbuf = swa_sample(qs, ks, vs, cache_swa_k[j], cache_swa_v[j], swa_sink[j])
            ms = os_ @ swa_wo[j]
            wp = min(WINDOW, Tp)
            skp.append(kp[:, -wp:])
            svp.append(vp[:, -wp:])
            sks.append(kbuf)
            svs.append(vbuf)
        else:
            qp, kp, vp = ret_project(xp, ret_wq[j], ret_wk[j], ret_wv[j], pos_p)
            qs, ks, vs = ret_project(xs, ret_wq[j], ret_wk[j], ret_wv[j], pos_s)
            Rp, op = retention_prompt(qp, kp, vp)
            Rs, o_s = retention_chunk(state_ret[j].astype(jnp.float32), qs, ks, vs)
            mp = ret_output(op, xp, ret_wg[j], ret_gn[j], ret_wo[j])
            ms = ret_output(o_s, xs, ret_wg[j], ret_gn[j], ret_wo[j])
            rsp.append(Rp)
            rss.append(Rs)
        hp = hp + mp
        hs = hs + ms
        hp = hp + sq_relu_mlp(rms_norm(hp, norm_mlp[i]), mlp_up[i], mlp_down[i])
        hs = hs + sq_relu_mlp(rms_norm(hs, norm_mlp[i]), mlp_up[i], mlp_down[i])
        hp = hp + ple_term(hp, p_prompt[i], norm_ple[i], ple_gate[i], ple_proj[i])
        hs = hs + ple_term(hs, p_sample[i], norm_ple[i], ple_gate[i], ple_proj[i])

    y_prompt = rms_norm(hp, norm_final)
    y_sample = rms_norm(hs, norm_final)
    fox_k_prompt = jnp.stack(fkp)
    fox_v_prompt = jnp.stack(fvp)
    fox_logf_prompt = jnp.stack(flp)
    fox_k_sample = jnp.stack(fks)
    fox_v_sample = jnp.stack(fvs)
    fox_logf_sample = jnp.stack(fls)
    swa_k_prompt = jnp.stack(skp)
    swa_v_prompt = jnp.stack(svp)
    swa_k_sample = jnp.stack(sks)
    swa_v_sample = jnp.stack(svs)
    ret_state_prompt = jnp.stack(rsp)
    ret_state_sample = jnp.stack(rss)
    return (y_prompt, y_sample, fox_k_prompt, fox_v_prompt, fox_logf_prompt, fox_k_sample, fox_v_sample,
            fox_logf_sample, swa_k_prompt, swa_v_prompt, swa_k_sample, swa_v_sample,
            ret_state_prompt, ret_state_sample)
```

```python
import functools

import jax
import jax.numpy as jnp
from jax import lax
from jax.experimental import pallas as pl
from jax.experimental.pallas import tpu as pltpu

F32 = jnp.float32
BF16 = jnp.bfloat16

D_MODEL = 1024
N_HEADS = 16
N_KV = 4
GROUP = 4
HEAD_DIM = 64
D_KV = N_KV * HEAD_DIM
ATTN_SCALE = HEAD_DIM ** -0.5
WINDOW = 128
PAGE = 128
RET_HEADS = 4
RET_DK = 256
RET_DV = 512
RET_CHUNK = 128
RET_THETA_BASE = 10000.0
D_FF = 4096
D_PLE = 256
EPS = 1e-6
N_MIXERS = 3

ROW_TILE = 512
FOX_BLOCK = 256
CUMSUM_BLOCK = 512
PAGES_PER_STEP = 8
NEW_PAD = 16
SWA_SEQS_PER_STEP = 8
VMEM_LIMIT = 56 * 1024 * 1024

NT_DIMS = (((1,), (1,)), ((), ()))


def _params(semantics):
    return pltpu.CompilerParams(dimension_semantics=semantics, vmem_limit_bytes=VMEM_LIMIT)


def _resident(shape):
    zeros = (0,) * len(shape)
    return pl.BlockSpec(shape, lambda *_: zeros, pipeline_mode=pl.Buffered(1))


def _rms(x, g):
    ms = jnp.mean(x * x, axis=-1, keepdims=True)
    return x * lax.rsqrt(ms + EPS) * g


def _dot(a, b):
    return jnp.dot(a, b, preferred_element_type=F32)


def _dot_nt(a, b):
    return lax.dot_general(a, b, NT_DIMS, preferred_element_type=F32)


def _split3(x):
    hi = x.astype(BF16)
    r = x - hi.astype(F32)
    mid = r.astype(BF16)
    lo = (r - mid.astype(F32)).astype(BF16)
    return hi, mid, lo


def _eye(n, dtype):
    r = lax.broadcasted_iota(jnp.int32, (n, n), 0)
    c = lax.broadcasted_iota(jnp.int32, (n, n), 1)
    return (r == c).astype(dtype)


def _upper_tri(n):
    r = lax.broadcasted_iota(jnp.int32, (n, n), 0)
    c = lax.broadcasted_iota(jnp.int32, (n, n), 1)
    return (r <= c).astype(BF16)


def _attn_proj_kernel(*refs, has_f):
    if has_f:
        h_ref, g_ref, w_ref, bf_ref, q_ref, k_ref, v_ref, kb_ref, vb_ref, lf_ref = refs
    else:
        h_ref, g_ref, w_ref, q_ref, k_ref, v_ref, kb_ref, vb_ref = refs
    xn = _rms(h_ref[...], g_ref[...]).astype(BF16)
    y = _dot(xn, w_ref[...])
    q_ref[...] = (y[:, :D_MODEL] * ATTN_SCALE).astype(BF16)
    k = y[:, D_MODEL:D_MODEL + D_KV]
    v = y[:, D_MODEL + D_KV:D_MODEL + 2 * D_KV]
    k_ref[...] = k
    v_ref[...] = v
    kb_ref[...] = k.astype(BF16)
    vb_ref[...] = v.astype(BF16)
    if has_f:
        z = y[:, D_MODEL + 2 * D_KV:D_MODEL + 2 * D_KV + N_HEADS] + bf_ref[...]
        lf_ref[...] = jnp.minimum(z, 0.0) - jnp.log1p(jnp.exp(-jnp.abs(z)))


def _attn_proj(h, g, w, bf=None):
    m = h.shape[0]
    has_f = bf is not None
    row = lambda n: pl.BlockSpec((ROW_TILE, n), lambda i: (i, 0))
    in_specs = [row(D_MODEL), _resident((1, D_MODEL)), _resident(w.shape)]
    args = [h, g, w]
    out_shape = [jax.ShapeDtypeStruct((m, D_MODEL), BF16),
                 jax.ShapeDtypeStruct((m, D_KV), F32), jax.ShapeDtypeStruct((m, D_KV), F32),
                 jax.ShapeDtypeStruct((m, D_KV), BF16), jax.ShapeDtypeStruct((m, D_KV), BF16)]
    out_specs = [row(D_MODEL), row(D_KV), row(D_KV), row(D_KV), row(D_KV)]
    if has_f:
        in_specs.append(_resident((1, N_HEADS)))
        args.append(bf)
        out_shape.append(jax.ShapeDtypeStruct((m, N_HEADS), F32))
        out_specs.append(row(N_HEADS))
    return pl.pallas_call(
        functools.partial(_attn_proj_kernel, has_f=has_f),
        grid=(m // ROW_TILE,), in_specs=in_specs, out_specs=out_specs, out_shape=out_shape,
        compiler_params=_params(("parallel",)), name="attn_proj")(*args)


def _cumsum_t_kernel(lf_ref, out_ref, carry):
    @pl.when(pl.program_id(1) == 0)
    def _():
        carry[...] = jnp.zeros_like(carry)

    tri = _upper_tri(CUMSUM_BLOCK)
    eye = _eye(N_HEADS, BF16)
    c = carry[...]
    for part in _split3(lf_ref[...]):
        part_t = _dot_nt(eye, part).astype(BF16)
        c = c + _dot(part_t, tri)
    out_ref[...] = c
    carry[...] = c[:, CUMSUM_BLOCK - 1:CUMSUM_BLOCK]


def _cumsum_t(lf, batch, t):
    nblk = t // CUMSUM_BLOCK
    return pl.pallas_call(
        _cumsum_t_kernel, grid=(batch, nblk),
        in_specs=[pl.BlockSpec((CUMSUM_BLOCK, N_HEADS), lambda b, c: (b * nblk + c, 0))],
        out_specs=pl.BlockSpec((None, N_HEADS, CUMSUM_BLOCK), lambda b, c: (b, 0, c)),
        out_shape=jax.ShapeDtypeStruct((batch, N_HEADS, t), F32),
        scratch_shapes=[pltpu.VMEM((N_HEADS, 1), F32)],
        compiler_params=_params(("parallel", "arbitrary")), name="fox_cumsum")(lf)


def _fox_prompt_kernel(q_ref, k_ref, v_ref, c_ref, o_ref, m_s, l_s, acc_s):
    blk = FOX_BLOCK
    i = pl.program_id(1)
    row = lax.broadcasted_iota(jnp.int32, (blk, blk), 0)
    col = lax.broadcasted_iota(jnp.int32, (blk, blk), 1)
    causal = col <= row

    for kv in range(N_KV):
        lanes = slice(kv * HEAD_DIM, (kv + 1) * HEAD_DIM)
        q_st = jnp.concatenate(
            [q_ref[:, (kv * GROUP + g) * HEAD_DIM:(kv * GROUP + g + 1) * HEAD_DIM] for g in range(GROUP)],
            axis=0)
        m_s[...] = jnp.full_like(m_s, -jnp.inf)
        l_s[...] = jnp.zeros_like(l_s)
        acc_s[...] = jnp.zeros_like(acc_s)

        def step(j, masked):
            start = pl.multiple_of(j * blk, blk)
            ks = k_ref[pl.ds(start, blk), lanes]
            vs = v_ref[pl.ds(start, blk), lanes]
            s = _dot_nt(q_st, ks)
            ps = []
            for g in range(GROUP):
                head = kv * GROUP + g
                rows = slice(g * blk, (g + 1) * blk)
                sg = s[rows] - c_ref[head:head + 1, pl.ds(start, blk)]
                if masked:
                    sg = jnp.where(causal, sg, -jnp.inf)
                m_old = m_s[rows]
                m_new = jnp.maximum(m_old, jnp.max(sg, axis=-1, keepdims=True))
                p = jnp.exp(sg - m_new)
                alpha = jnp.exp(m_old - m_new)
                l_s[rows] = alpha * l_s[rows] + jnp.sum(p, axis=-1, keepdims=True)
                m_s[rows] = m_new
                acc_s[rows] = alpha * acc_s[rows]
                ps.append(p.astype(BF16))
            acc_s[...] += _dot(jnp.concatenate(ps, axis=0), vs)

        def full_step(j, carry):
            step(j, False)
            return carry

        lax.fori_loop(0, i, full_step, 0)
        step(i, True)
        out = acc_s[...] / l_s[...]
        for g in range(GROUP):
            head = kv * GROUP + g
            o_ref[:, head * HEAD_DIM:(head + 1) * HEAD_DIM] = out[g * blk:(g + 1) * blk].astype(BF16)


def _fox_prompt(q, kb, vb, c_t, batch, t):
    nq = t // FOX_BLOCK
    return pl.pallas_call(
        _fox_prompt_kernel, grid=(batch, nq),
        in_specs=[pl.BlockSpec((FOX_BLOCK, D_MODEL), lambda b, i: (b * nq + i, 0)),
                  pl.BlockSpec((t, D_KV), lambda b, i: (b, 0)),
                  pl.BlockSpec((t, D_KV), lambda b, i: (b, 0)),
                  pl.BlockSpec((None, N_HEADS, t), lambda b, i: (b, 0, 0))],
        out_specs=pl.BlockSpec((FOX_BLOCK, D_MODEL), lambda b, i: (b * nq + i, 0)),
        out_shape=jax.ShapeDtypeStruct((batch * t, D_MODEL), BF16),
        scratch_shapes=[pltpu.VMEM((GROUP * FOX_BLOCK, 1), F32), pltpu.VMEM((GROUP * FOX_BLOCK, 1), F32),
                        pltpu.VMEM((GROUP * FOX_BLOCK, HEAD_DIM), F32)],
        compiler_params=_params(("parallel", "arbitrary")), name="fox_prompt")(q, kb, vb, c_t)


def _row_cumsum(lf, expand, tri, carry, width, n):
    rows = expand.shape[0]
    parts = [_dot_nt(expand, p).astype(BF16) for p in _split3(lf)]
    stacked = jnp.concatenate(
        [p[:, i * width:(i + 1) * width] for i in range(n) for p in parts], axis=0)
    cs = _dot(stacked, tri)
    out = []
    for i in range(n):
        base = 3 * rows * i
        g = cs[base:base + rows] + cs[base + rows:base + 2 * rows] + cs[base + 2 * rows:base + 3 * rows]
        g = g + carry
        carry = g[:, width - 1:width]
        out.append(g)
    return out, carry


def _softmax_step(s, v, m_s, l_s, acc_s):
    m_old = m_s[...]
    m_new = jnp.maximum(m_old, jnp.max(s, axis=-1, keepdims=True))
    p = jnp.exp(s - m_new)
    alpha = jnp.exp(m_old - m_new)
    l_s[...] = alpha * l_s[...] + jnp.sum(p, axis=-1, keepdims=True)
    m_s[...] = m_new
    acc_s[...] = alpha * acc_s[...] + _dot(p.astype(BF16), v)


def _fox_sample_kernel(pt_ref, q_ref, kn_ref, vn_ref, lfn_ref, *rest):
    del pt_ref
    pp = PAGES_PER_STEP
    k_refs, v_refs, lf_refs = rest[:pp], rest[pp:2 * pp], rest[2 * pp:3 * pp]
    o_ref, m_s, l_s, acc_s, g_s, kcat, vcat = rest[3 * pp:]
    step = pl.program_id(1)
    rows = N_HEADS * 4

    @pl.when(step == 0)
    def _():
        m_s[...] = jnp.full_like(m_s, -jnp.inf)
        l_s[...] = jnp.zeros_like(l_s)
        acc_s[...] = jnp.zeros_like(acc_s)
        g_s[...] = jnp.zeros_like(g_s)

    r = lax.broadcasted_iota(jnp.int32, (rows, N_HEADS), 0)
    hcol = lax.broadcasted_iota(jnp.int32, (rows, N_HEADS), 1)
    expand = (r // 4 == hcol).astype(BF16)
    q = q_ref[...]

    for i in range(pp):
        kcat[i * PAGE:(i + 1) * PAGE, :] = k_refs[i][...].astype(BF16)
        vcat[i * PAGE:(i + 1) * PAGE, :] = v_refs[i][...].astype(BF16)
    lf = jnp.concatenate([lf_refs[i][...] for i in range(pp)], axis=0)
    gs, carry = _row_cumsum(lf, expand, _upper_tri(PAGE), g_s[...], PAGE, pp)
    g_s[...] = carry
    s = _dot_nt(q, kcat[...]) - jnp.concatenate(gs, axis=1)
    _softmax_step(s, vcat[...], m_s, l_s, acc_s)

    @pl.when(step == pl.num_programs(1) - 1)
    def _():
        gn, _ = _row_cumsum(lfn_ref[...], expand, _upper_tri(NEW_PAD), carry, NEW_PAD, 1)
        s_new = _dot_nt(q, kn_ref[...].astype(BF16)) - gn[0]
        u = lax.broadcasted_iota(jnp.int32, (rows, NEW_PAD), 1)
        t = lax.broadcasted_iota(jnp.int32, (rows, NEW_PAD), 0) % 4
        s_new = jnp.where(u <= t, s_new, -jnp.inf)
        _softmax_step(s_new, vn_ref[...].astype(BF16), m_s, l_s, acc_s)
        o_ref[...] = acc_s[...] / l_s[...]


def _fox_sample(q_bd, k_new, v_new, lf_new, cache_k, cache_v, cache_lf, page_ids):
    nseq, npages = page_ids.shape
    pp = PAGES_PER_STEP
    rows = N_HEADS * 4
    per_seq = lambda *shape: pl.BlockSpec((None,) + shape, lambda b, s, pt: (b, 0, 0))

    def page(width, i):
        return pl.BlockSpec((None, PAGE, width), lambda b, s, pt: (pt[b, s * pp + i], 0, 0))

    in_specs = ([per_seq(rows, D_KV), per_seq(NEW_PAD, D_KV), per_seq(NEW_PAD, D_KV), per_seq(NEW_PAD, N_HEADS)]
                + [page(D_KV, i) for i in range(pp)] + [page(D_KV, i) for i in range(pp)]
                + [page(N_HEADS, i) for i in range(pp)])
    grid_spec = pltpu.PrefetchScalarGridSpec(
        num_scalar_prefetch=1, grid=(nseq, npages // pp), in_specs=in_specs,
        out_specs=per_seq(rows, D_KV),
        scratch_shapes=[pltpu.VMEM((rows, 1), F32), pltpu.VMEM((rows, 1), F32), pltpu.VMEM((rows, D_KV), F32),
                        pltpu.VMEM((rows, 1), F32), pltpu.VMEM((pp * PAGE, D_KV), BF16),
                        pltpu.VMEM((pp * PAGE, D_KV), BF16)])
    return pl.pallas_call(
        _fox_sample_kernel, grid_spec=grid_spec,
        out_shape=jax.ShapeDtypeStruct((nseq, rows, D_KV), F32),
        compiler_params=_params(("parallel", "arbitrary")), name="fox_sample")(
            page_ids, q_bd, k_new, v_new, lf_new, *([cache_k] * pp), *([cache_v] * pp), *([cache_lf] * pp))


def _swa_prompt_kernel(q_ref, kp_ref, kc_ref, vp_ref, vc_ref, sink_ref, o_ref):
    w = WINDOW
    i = pl.program_id(1)
    row = lax.broadcasted_iota(jnp.int32, (w, 2 * w), 0)
    col = lax.broadcasted_iota(jnp.int32, (w, 2 * w), 1)
    valid = (col > row) & (col <= row + w) & ((col >= w) | (i > 0))
    for kv in range(N_KV):
        lanes = slice(kv * HEAD_DIM, (kv + 1) * HEAD_DIM)
        q_st = jnp.concatenate(
            [q_ref[:, (kv * GROUP + g) * HEAD_DIM:(kv * GROUP + g + 1) * HEAD_DIM] for g in range(GROUP)],
            axis=0)
        kk = jnp.concatenate([kp_ref[:, lanes], kc_ref[:, lanes]], axis=0)
        vv = jnp.concatenate([vp_ref[:, lanes], vc_ref[:, lanes]], axis=0)
        s = _dot_nt(q_st, kk)
        ps, dens = [], []
        for g in range(GROUP):
            head = kv * GROUP + g
            sg = jnp.where(valid, s[g * w:(g + 1) * w], -jnp.inf)
            sink = sink_ref[0:1, head:head + 1]
            m = jnp.maximum(jnp.max(sg, axis=-1, keepdims=True), sink)
            p = jnp.exp(sg - m)
            dens.append(jnp.sum(p, axis=-1, keepdims=True) + jnp.exp(sink - m))
            ps.append(p.astype(BF16))
        out = _dot(jnp.concatenate(ps, axis=0), vv) / jnp.concatenate(dens, axis=0)
        for g in range(GROUP):
            head = kv * GROUP + g
            o_ref[:, head * HEAD_DIM:(head + 1) * HEAD_DIM] = out[g * w:(g + 1) * w].astype(BF16)


def _swa_prompt(q, kb, vb, sink, batch, t):
    nb = t // WINDOW
    cur = lambda n: pl.BlockSpec((WINDOW, n), lambda b, i: (b * nb + i, 0))
    prev = lambda n: pl.BlockSpec((WINDOW, n), lambda b, i: (b * nb + jnp.maximum(i - 1, 0), 0))
    return pl.pallas_call(
        _swa_prompt_kernel, grid=(batch, nb),
        in_specs=[cur(D_MODEL), prev(D_KV), cur(D_KV), prev(D_KV), cur(D_KV), _resident((1, N_HEADS))],
        out_specs=cur(D_MODEL),
        out_shape=jax.ShapeDtypeStruct((batch * t, D_MODEL), BF16),
        compiler_params=_params(("parallel", "arbitrary")), name="swa_prompt")(q, kb, kb, vb, vb, sink)


def _swa_sample_kernel(q_ref, ck_ref, cv_ref, kn_ref, vn_ref, sink_ref, o_ref, kbuf_ref, vbuf_ref, *, n_new):
    rows = N_HEADS * 4
    key = lax.broadcasted_iota(jnp.int32, (rows, WINDOW), 1)
    t_c = lax.broadcasted_iota(jnp.int32, (rows, WINDOW), 0) % 4
    u = lax.broadcasted_iota(jnp.int32, (rows, NEW_PAD), 1)
    t_n = lax.broadcasted_iota(jnp.int32, (rows, NEW_PAD), 0) % 4
    sink = sink_ref[...]
    for b in range(SWA_SEQS_PER_STEP):
        q = q_ref[b]
        ck, cv, kn, vn = ck_ref[b], cv_ref[b], kn_ref[b], vn_ref[b]
        s_c = jnp.where(key > t_c, _dot_nt(q, ck.astype(BF16)), -jnp.inf)
        s_n = jnp.where(u <= t_n, _dot_nt(q, kn.astype(BF16)), -jnp.inf)
        m = jnp.maximum(jnp.maximum(jnp.max(s_c, axis=-1, keepdims=True),
                                    jnp.max(s_n, axis=-1, keepdims=True)), sink)
        p_c = jnp.exp(s_c - m)
        p_n = jnp.exp(s_n - m)
        den = (jnp.sum(p_c, axis=-1, keepdims=True) + jnp.sum(p_n, axis=-1, keepdims=True)
               + jnp.exp(sink - m))
        o = _dot(p_c.astype(BF16), cv.astype(BF16)) + _dot(p_n.astype(BF16), vn.astype(BF16))
        o_ref[b] = o / den
        kbuf_ref[b, 0:WINDOW - n_new, :] = ck[n_new:, :]
        kbuf_ref[b, WINDOW - n_new:WINDOW, :] = kn[0:n_new, :]
        vbuf_ref[b, 0:WINDOW - n_new, :] = cv[n_new:, :]
        vbuf_ref[b, WINDOW - n_new:WINDOW, :] = vn[0:n_new, :]


def _swa_sample(q_bd, cache_k, cache_v, k_new, v_new, sink_rows, n_new):
    nseq = q_bd.shape[0]
    rows = N_HEADS * 4
    sb = SWA_SEQS_PER_STEP
    blk = lambda a, b: pl.BlockSpec((sb, a, b), lambda i: (i, 0, 0))
    return pl.pallas_call(
        functools.partial(_swa_sample_kernel, n_new=n_new), grid=(nseq // sb,),
        in_specs=[blk(rows, D_KV), blk(WINDOW, D_KV), blk(WINDOW, D_KV), blk(NEW_PAD, D_KV), blk(NEW_PAD, D_KV),
                  _resident((rows, 1))],
        out_specs=[blk(rows, D_KV), blk(WINDOW, D_KV), blk(WINDOW, D_KV)],
        out_shape=[jax.ShapeDtypeStruct((nseq, rows, D_KV), F32),
                   jax.ShapeDtypeStruct((nseq, WINDOW, D_KV), F32),
                   jax.ShapeDtypeStruct((nseq, WINDOW, D_KV), F32)],
        compiler_params=_params(("parallel",)), name="swa_sample")(
            q_bd, cache_k, cache_v, k_new, v_new, sink_rows)


def _ret_proj_kernel(h_ref, g_ref, w_ref, cos_ref, sin_ref, q_ref, k_ref, v_ref, gate_ref):
    xn = _rms(h_ref[...], g_ref[...]).astype(BF16)
    cos = jnp.concatenate([cos_ref[...]] * RET_HEADS, axis=1)
    sin = jnp.concatenate([sin_ref[...]] * RET_HEADS, axis=1)
    d = D_MODEL
    q_ref[...] = _dot(xn, w_ref[:, 0:d]) * cos + _dot(xn, w_ref[:, d:2 * d]) * sin
    k_scale = RET_DK ** -0.5
    k_ref[...] = (_dot(xn, w_ref[:, 2 * d:3 * d]) * k_scale) * cos + (_dot(xn, w_ref[:, 3 * d:4 * d]) * k_scale) * sin
    v_ref[...] = _dot(xn, w_ref[:, 4 * d:6 * d])
    z = _dot(xn, w_ref[:, 6 * d:8 * d])
    gate_ref[...] = z / (1.0 + jnp.exp(-z))


def _ret_proj(h, g, w, cos_tab, sin_tab, prompt_tiles, tiles_per_seq):
    m = h.shape[0]
    row = lambda n: pl.BlockSpec((ROW_TILE, n), lambda i: (i, 0))
    tab = pl.BlockSpec((ROW_TILE, RET_DK),
                       lambda i: (jnp.where(i < prompt_tiles, i % tiles_per_seq, tiles_per_seq), 0))
    dv = RET_HEADS * RET_DV
    return pl.pallas_call(
        _ret_proj_kernel, grid=(m // ROW_TILE,),
        in_specs=[row(D_MODEL), _resident((1, D_MODEL)), _resident(w.shape), tab, tab],
        out_specs=[row(D_MODEL), row(D_MODEL), row(dv), row(dv)],
        out_shape=[jax.ShapeDtypeStruct((m, D_MODEL), F32), jax.ShapeDtypeStruct((m, D_MODEL), F32),
                   jax.ShapeDtypeStruct((m, dv), F32), jax.ShapeDtypeStruct((m, dv), F32)],
        compiler_params=_params(("parallel",)), name="ret_proj")(h, g, w, cos_tab, sin_tab)


def _ret_chunk_kernel(*refs, has_init):
    if has_init:
        q_ref, k_ref, v_ref, gate_ref, gn_ref, decay_ref, xi_ref, zeta_ref, gl_ref, r0_ref, y_ref, r_ref, r_s = refs
    else:
        q_ref, k_ref, v_ref, gate_ref, gn_ref, decay_ref, xi_ref, zeta_ref, gl_ref, y_ref, r_ref, r_s = refs

    @pl.when(pl.program_id(2) == 0)
    def _():
        r_s[...] = r0_ref[...] if has_init else jnp.zeros_like(r_s)

    q = q_ref[...]
    k = k_ref[...]
    vb = v_ref[...].astype(BF16)
    r = r_s[...]
    s = _dot_nt(q.astype(BF16), k.astype(BF16)) * decay_ref[...]
    o = _dot(s.astype(BF16), vb) + _dot((q * xi_ref[...]).astype(BF16), r.astype(BF16))
    kz = (k * zeta_ref[...]).astype(BF16)
    kz_t = _dot_nt(_eye(RET_DK, BF16), kz).astype(BF16)
    r_new = r * gl_ref[...] + _dot(kz_t, vb)
    r_s[...] = r_new
    r_ref[...] = r_new

    mu = jnp.mean(o, axis=-1, keepdims=True)
    oc = o - mu
    var = jnp.mean(oc * oc, axis=-1, keepdims=True)
    y = oc * lax.rsqrt(var + EPS) * gn_ref[...]
    y_ref[...] = (gate_ref[...] * y).astype(BF16)


def _ret_chunks(q, k, v, gate, gn, tables, batch, nchunk, chunk, r0=None):
    decay, xi, zeta, gl = tables
    has_init = r0 is not None
    tok = lambda n: pl.BlockSpec((chunk, n), lambda b, h, c: (b * nchunk + c, h))
    per_head = lambda a, b_: pl.BlockSpec((None, a, b_), lambda b, h, c: (h, 0, 0))
    state = pl.BlockSpec((None, None, RET_DK, RET_DV), lambda b, h, c: (b, h, 0, 0))
    in_specs = [tok(RET_DK), tok(RET_DK), tok(RET_DV), tok(RET_DV),
                pl.BlockSpec((1, RET_DV), lambda b, h, c: (0, h)),
                per_head(chunk, chunk), per_head(chunk, 1), per_head(chunk, 1), per_head(1, 1)]
    args = [q, k, v, gate, gn, decay, xi, zeta, gl]
    if has_init:
        in_specs.append(state)
        args.append(r0)
    return pl.pallas_call(
        functools.partial(_ret_chunk_kernel, has_init=has_init), grid=(batch, RET_HEADS, nchunk),
        in_specs=in_specs, out_specs=[tok(RET_DV), state],
        out_shape=[jax.ShapeDtypeStruct((batch * nchunk * chunk, RET_HEADS * RET_DV), BF16),
                   jax.ShapeDtypeStruct((batch, RET_HEADS, RET_DK, RET_DV), F32)],
        scratch_shapes=[pltpu.VMEM((RET_DK, RET_DV), F32)],
        compiler_params=_params(("parallel", "parallel", "arbitrary")), name="ret_chunks")(*args)


def _ret_tables(chunk, n_valid):
    lg = jnp.log(1.0 - 2.0 ** (-5.0 - jnp.arange(RET_HEADS, dtype=F32)))
    idx = jnp.arange(chunk, dtype=F32)
    rel = idx[:, None] - idx[None, :]
    decay = jnp.where(rel >= 0, jnp.exp(lg[:, None, None] * jnp.maximum(rel, 0.0)), 0.0)
    xi = jnp.exp(lg[:, None] * (idx[None, :] + 1.0))[:, :, None]
    zeta = jnp.where(idx[None, :] < n_valid, jnp.exp(lg[:, None] * (n_valid - 1.0 - idx[None, :])), 0.0)[:, :, None]
    gl = jnp.exp(lg * n_valid)[:, None, None]
    return decay, xi, zeta, gl


def _rotation_tables(pos):
    freqs = 1.0 / (RET_THETA_BASE ** jnp.linspace(0.0, 1.0, RET_DK // 2, dtype=F32))
    ang = pos.astype(F32)[:, None] * freqs[None, :]
    cos = jnp.repeat(jnp.cos(ang), 2, axis=1)
    sign = jnp.tile(jnp.array([-1.0, 1.0], F32), RET_DK // 2)
    sin = jnp.repeat(jnp.sin(ang), 2, axis=1) * sign[None, :]
    return cos, sin


def _swap_pairs(w):
    d_in, d_out = w.shape
    return w.reshape(d_in, d_out // 2, 2)[:, :, ::-1].reshape(d_in, d_out)


def _post_kernel(o_ref, h_ref, p_ref, wo_ref, gm_ref, up_ref, dn_ref, gp_ref, wg_ref, wp_ref, gf_ref, out_ref,
                 *, final):
    h = h_ref[...] + _dot(o_ref[...], wo_ref[...])
    xn = _rms(h, gm_ref[...]).astype(BF16)
    mlp = jnp.zeros_like(h)
    for c in range(D_FF // D_MODEL):
        cols = slice(c * D_MODEL, (c + 1) * D_MODEL)
        u = jnp.maximum(_dot(xn, up_ref[:, cols]), 0.0)
        mlp = mlp + _dot((u * u).astype(BF16), dn_ref[cols, :])
    h = h + mlp
    xn = _rms(h, gp_ref[...]).astype(BF16)
    gate = 1.0 / (1.0 + jnp.exp(-_dot(xn, wg_ref[...])))
    h = h + gate * _dot(p_ref[...].astype(BF16), wp_ref[...])
    if final:
        h = _rms(h, gf_ref[...])
    out_ref[...] = h


def _post(o, h, p, wo, gm, up, dn, gp, wg, wp, gf, final):
    m = h.shape[0]
    row = lambda n: pl.BlockSpec((ROW_TILE, n), lambda i: (i, 0))
    vec = _resident((1, D_MODEL))
    return pl.pallas_call(
        functools.partial(_post_kernel, final=final), grid=(m // ROW_TILE,),
        in_specs=[row(o.shape[1]), row(D_MODEL), row(D_PLE), _resident(wo.shape), vec, _resident(up.shape),
                  _resident(dn.shape), vec, _resident(wg.shape), _resident(wp.shape), vec],
        out_specs=row(D_MODEL), out_shape=jax.ShapeDtypeStruct((m, D_MODEL), F32),
        compiler_params=_params(("parallel",)), name="post_mixer")(o, h, p, wo, gm, up, dn, gp, wg, wp, gf)


def _block_diag_queries(q, nseq, t):
    q5 = q.reshape(nseq, t, N_KV, GROUP, HEAD_DIM).transpose(0, 2, 3, 1, 4)
    q4 = q5.reshape(nseq, N_KV, GROUP * t, 1, HEAD_DIM)
    eye = jnp.eye(N_KV, dtype=q.dtype)[None, :, None, :, None]
    return (q4 * eye).reshape(nseq, N_KV * GROUP * t, D_KV)


def _block_diag_outputs(o, nseq, t):
    o5 = o.reshape(nseq, N_KV, GROUP * t, N_KV, HEAD_DIM)
    sel = jnp.stack([o5[:, kv, :, kv, :] for kv in range(N_KV)], axis=1)
    return sel.reshape(nseq, N_KV, GROUP, t, HEAD_DIM).transpose(0, 3, 1, 2, 4).reshape(nseq * t, D_MODEL)


def _pad_tokens(x, nseq, t, width):
    return jnp.pad(x.reshape(nseq, t, width), ((0, 0), (0, NEW_PAD - t), (0, 0)))


def kernel(x_prompt, x_sample, cache_fox_k, cache_fox_v, cache_fox_logf, cache_swa_k, cache_swa_v, state_ret, page_table, p_prompt, p_sample, fox_wq, fox_wk, fox_wv, fox_wf, fox_bf, fox_wo, swa_wq, swa_wk, swa_wv, swa_sink, swa_wo, ret_wq, ret_wk, ret_wv, ret_wg, ret_gn, ret_wo, norm_mix, norm_mlp, mlp_up, mlp_down, norm_ple, ple_gate, ple_proj, norm_final):
    batch, t_p, d = x_prompt.shape
    nseq, t_s, _ = x_sample.shape
    depth = norm_mix.shape[0]
    m_p = batch * t_p
    m_s = nseq * t_s
    n_pool = cache_fox_k.shape[1]
    past = page_table.shape[1] * PAGE
    assert d == D_MODEL and (m_p + m_s) % ROW_TILE == 0 and m_p % ROW_TILE == 0 and m_s == ROW_TILE
    assert t_p % ROW_TILE == 0 and t_s <= NEW_PAD and N_HEADS * t_s == 64

    h = jnp.concatenate([x_prompt.reshape(m_p, d), x_sample.reshape(m_s, d)], axis=0)
    p_all = jnp.concatenate([p_prompt.reshape(depth, m_p, D_PLE), p_sample.reshape(depth, m_s, D_PLE)], axis=1)
    vec = lambda a: a.reshape(1, -1)

    fox_cache_k = cache_fox_k.reshape(-1, PAGE, D_KV)
    fox_cache_v = cache_fox_v.reshape(-1, PAGE, D_KV)
    fox_cache_lf = cache_fox_logf.reshape(-1, PAGE, N_HEADS)

    outs = {name: [] for name in ("fkp", "fvp", "flp", "fks", "fvs", "fls", "skp", "svp", "sks", "svs", "rsp", "rss")}

    for i in range(depth):
        kind, j = i % N_MIXERS, i // N_MIXERS
        g_mix = vec(norm_mix[i])
        if kind == 0:
            wf = jnp.pad(fox_wf[j], ((0, 0), (0, 128 - N_HEADS)))
            w = jnp.concatenate([fox_wq[j], fox_wk[j], fox_wv[j], wf], axis=1).astype(BF16)
            q, k, v, kb, vb, lf = _attn_proj(h, g_mix, w, vec(fox_bf[j]))
            c_t = _cumsum_t(lf, batch, t_p)
            o_p = _fox_prompt(q, kb, vb, c_t, batch, t_p)
            q_bd = _block_diag_queries(q[m_p:], nseq, t_s)
            o_bd = _fox_sample(q_bd, _pad_tokens(k[m_p:], nseq, t_s, D_KV), _pad_tokens(v[m_p:], nseq, t_s, D_KV),
                               _pad_tokens(lf[m_p:], nseq, t_s, N_HEADS),
                               fox_cache_k, fox_cache_v, fox_cache_lf, page_table + j * n_pool)
            o_s = _block_diag_outputs(o_bd, nseq, t_s).astype(BF16)
            o = jnp.concatenate([o_p, o_s], axis=0)
            wo = fox_wo[j]
            outs["fkp"].append(k[:m_p].reshape(batch, t_p, N_KV, HEAD_DIM))
            outs["fvp"].append(v[:m_p].reshape(batch, t_p, N_KV, HEAD_DIM))
            outs["flp"].append(lf[:m_p].reshape(batch, t_p, N_HEADS))
            outs["fks"].append(k[m_p:].reshape(nseq, t_s, N_KV, HEAD_DIM))
            outs["fvs"].append(v[m_p:].reshape(nseq, t_s, N_KV, HEAD_DIM))
            outs["fls"].append(lf[m_p:].reshape(nseq, t_s, N_HEADS))
        elif kind == 1:
            w = jnp.concatenate([swa_wq[j], swa_wk[j], swa_wv[j]], axis=1).astype(BF16)
            q, k, v, kb, vb = _attn_proj(h, g_mix, w)
            o_p = _swa_prompt(q, kb, vb, vec(swa_sink[j]), batch, t_p)
            q_bd = _block_diag_queries(q[m_p:], nseq, t_s)
            sink_rows = jnp.repeat(swa_sink[j], t_s).reshape(N_HEADS * t_s, 1)
            win = cache_swa_k.shape[2]
            o_bd, kbuf, vbuf = _swa_sample(
                q_bd, cache_swa_k[j].reshape(nseq, win, D_KV), cache_swa_v[j].reshape(nseq, win, D_KV),
                _pad_tokens(k[m_p:], nseq, t_s, D_KV), _pad_tokens(v[m_p:], nseq, t_s, D_KV), sink_rows, t_s)
            o_s = _block_diag_outputs(o_bd, nseq, t_s).astype(BF16)
            o = jnp.concatenate([o_p, o_s], axis=0)
            wo = swa_wo[j]
            wp = min(WINDOW, t_p)
            outs["skp"].append(k[:m_p].reshape(batch, t_p, N_KV, HEAD_DIM)[:, -wp:])
            outs["svp"].append(v[:m_p].reshape(batch, t_p, N_KV, HEAD_DIM)[:, -wp:])
            outs["sks"].append(kbuf.reshape(nseq, win, N_KV, HEAD_DIM))
            outs["svs"].append(vbuf.reshape(nseq, win, N_KV, HEAD_DIM))
        else:
            w = jnp.concatenate([ret_wq[j], _swap_pairs(ret_wq[j]), ret_wk[j], _swap_pairs(ret_wk[j]),
                                 ret_wv[j], ret_wg[j]], axis=1).astype(BF16)
            cos_p, sin_p = _rotation_tables(jnp.arange(t_p))
            cos_s, sin_s = _rotation_tables(past + jnp.arange(t_s))
            cos_tab = jnp.concatenate([cos_p, jnp.tile(cos_s, (nseq, 1))], axis=0)
            sin_tab = jnp.concatenate([sin_p, jnp.tile(sin_s, (nseq, 1))], axis=0)
            qr, kr, v, gate = _ret_proj(h, g_mix, w, cos_tab, sin_tab, m_p // ROW_TILE, t_p // ROW_TILE)
            gn = vec(ret_gn[j])
            y_p, r_p = _ret_chunks(qr, kr, v, gate, gn, _ret_tables(RET_CHUNK, RET_CHUNK),
                                   batch, t_p // RET_CHUNK, RET_CHUNK)
            dv = RET_HEADS * RET_DV
            pad = lambda a, n: _pad_tokens(a[m_p:], nseq, t_s, n).reshape(nseq * NEW_PAD, n)
            y_s, r_s = _ret_chunks(pad(qr, D_MODEL), pad(kr, D_MODEL), pad(v, dv), pad(gate, dv), gn,
                                   _ret_tables(NEW_PAD, t_s), nseq, 1, NEW_PAD,
                                   r0=state_ret[j].astype(F32))
            y_s = y_s.reshape(nseq, NEW_PAD, dv)[:, :t_s].reshape(m_s, dv)
            o = jnp.concatenate([y_p, y_s], axis=0)
            wo = ret_wo[j]
            outs["rsp"].append(r_p)
            outs["rss"].append(r_s)
        h = _post(o, h, p_all[i], wo.astype(BF16), vec(norm_mlp[i]), mlp_up[i].astype(BF16),
                  mlp_down[i].astype(BF16), vec(norm_ple[i]), ple_gate[i].astype(BF16),
                  ple_proj[i].astype(BF16), vec(norm_final), final=(i == depth - 1))

    y_prompt = h[:m_p].reshape(batch, t_p, d)
    y_sample = h[m_p:].reshape(nseq, t_s, d)
    st = lambda name: jnp.stack(outs[name])
    return (y_prompt, y_sample, st("fkp"), st("fvp"), st("flp"), st("fks"), st("fvs"), st("fls"),
            st("skp"), st("svp"), st("sks"), st("svs"), st("rsp"), st("rss"))
```

```python
import functools

import jax
import jax.numpy as jnp
from jax import lax
from jax.experimental import pallas as pl
from jax.experimental.pallas import tpu as pltpu

F32 = jnp.float32
BF16 = jnp.bfloat16

D_MODEL = 1024
N_HEADS = 16
N_KV = 4
GROUP = 4
HEAD_DIM = 64
D_KV = N_KV * HEAD_DIM
ATTN_SCALE = HEAD_DIM ** -0.5
WINDOW = 128
PAGE = 128
RET_HEADS = 4
RET_DK = 256
RET_DV = 512
RET_CHUNK = 128
RET_THETA_BASE = 10000.0
D_FF = 4096
D_PLE = 256
EPS = 1e-6
N_MIXERS = 3

LANES = 128
ROW_TILE = 512
FOX_Q_BLOCK = 256
FOX_K_BLOCK = 512
FOX_KV_PER_LOOP = 2
FOX_ROW_CHUNK = 256
CUMSUM_BLOCK = 512
PAGES_PER_STEP = 16
NEW_PAD = 16
SWA_SEQS_PER_STEP = 8
VMEM_LIMIT = 56 * 1024 * 1024

FOX_BIAS_OFFSET = HEAD_DIM
FOX_ONES_ROW = HEAD_DIM

NT_DIMS = (((1,), (1,)), ((), ()))


def _params(semantics):
    return pltpu.CompilerParams(dimension_semantics=semantics, vmem_limit_bytes=VMEM_LIMIT)


def _resident(shape):
    zeros = (0,) * len(shape)
    return pl.BlockSpec(shape, lambda *_: zeros, pipeline_mode=pl.Buffered(1))


def _rms(x, g):
    ms = jnp.mean(x * x, axis=-1, keepdims=True)
    return x * lax.rsqrt(ms + EPS) * g


def _dot(a, b):
    return jnp.dot(a, b, preferred_element_type=F32)


def _dot_nt(a, b):
    return lax.dot_general(a, b, NT_DIMS, preferred_element_type=F32)


def _split3(x):
    hi = x.astype(BF16)
    r = x - hi.astype(F32)
    mid = r.astype(BF16)
    lo = (r - mid.astype(F32)).astype(BF16)
    return hi, mid, lo


def _eye(n, dtype):
    r = lax.broadcasted_iota(jnp.int32, (n, n), 0)
    c = lax.broadcasted_iota(jnp.int32, (n, n), 1)
    return (r == c).astype(dtype)


def _upper_tri(n):
    r = lax.broadcasted_iota(jnp.int32, (n, n), 0)
    c = lax.broadcasted_iota(jnp.int32, (n, n), 1)
    return (r <= c).astype(BF16)


def _log_sigmoid(z):
    return jnp.minimum(z, 0.0) - jnp.log1p(jnp.exp(-jnp.abs(z)))


def _swa_proj_kernel(h_ref, g_ref, w_ref, q_ref, k_ref, v_ref, kb_ref, vb_ref):
    xn = _rms(h_ref[...], g_ref[...]).astype(BF16)
    y = _dot(xn, w_ref[...])
    q_ref[...] = (y[:, :D_MODEL] * ATTN_SCALE).astype(BF16)
    k = y[:, D_MODEL:D_MODEL + D_KV]
    v = y[:, D_MODEL + D_KV:D_MODEL + 2 * D_KV]
    k_ref[...] = k
    v_ref[...] = v
    kb_ref[...] = k.astype(BF16)
    vb_ref[...] = v.astype(BF16)


def _swa_proj(h, g, w):
    m = h.shape[0]
    row = lambda n: pl.BlockSpec((ROW_TILE, n), lambda i: (i, 0))
    return pl.pallas_call(
        _swa_proj_kernel, grid=(m // ROW_TILE,),
        in_specs=[row(D_MODEL), _resident((1, D_MODEL)), _resident(w.shape)],
        out_specs=[row(D_MODEL), row(D_KV), row(D_KV), row(D_KV), row(D_KV)],
        out_shape=[jax.ShapeDtypeStruct((m, D_MODEL), BF16),
                   jax.ShapeDtypeStruct((m, D_KV), F32), jax.ShapeDtypeStruct((m, D_KV), F32),
                   jax.ShapeDtypeStruct((m, D_KV), BF16), jax.ShapeDtypeStruct((m, D_KV), BF16)],
        compiler_params=_params(("parallel",)), name="swa_proj")(h, g, w)


def _fox_proj_kernel(h_ref, g_ref, wq_ref, sel_ref, wt_ref, bf_ref, q_ref, kt_ref, vt_ref, lft_ref):
    xn = _rms(h_ref[...], g_ref[...]).astype(BF16)
    q_ref[...] = (_dot(xn, wq_ref[...]) * ATTN_SCALE + sel_ref[...]).astype(BF16)
    t = _dot_nt(wt_ref[...], xn)
    kt_ref[...] = t[0:D_KV]
    vt_ref[...] = t[D_KV:2 * D_KV]
    lft_ref[...] = _log_sigmoid(t[2 * D_KV:2 * D_KV + N_HEADS] + bf_ref[...])


def _fox_proj(h, g, wq_pad, sel, w_t, bf_col):
    m = h.shape[0]
    row = lambda n: pl.BlockSpec((ROW_TILE, n), lambda i: (i, 0))
    col = lambda n: pl.BlockSpec((n, ROW_TILE), lambda i: (0, i))
    return pl.pallas_call(
        _fox_proj_kernel, grid=(m // ROW_TILE,),
        in_specs=[row(D_MODEL), _resident((1, D_MODEL)), _resident(wq_pad.shape), _resident(sel.shape),
                  _resident(w_t.shape), _resident(bf_col.shape)],
        out_specs=[row(N_HEADS * LANES), col(D_KV), col(D_KV), col(N_HEADS)],
        out_shape=[jax.ShapeDtypeStruct((m, N_HEADS * LANES), BF16),
                   jax.ShapeDtypeStruct((D_KV, m), F32), jax.ShapeDtypeStruct((D_KV, m), F32),
                   jax.ShapeDtypeStruct((N_HEADS, m), F32)],
        compiler_params=_params(("parallel",)), name="fox_proj")(h, g, wq_pad, sel, w_t, bf_col)


def _fox_query_weights(wq):
    d = wq.shape[0]
    w = jnp.pad(wq.reshape(d, N_HEADS, HEAD_DIM), ((0, 0), (0, 0), (0, LANES - HEAD_DIM)))
    lane = jnp.arange(LANES)[None, :]
    g = (jnp.arange(N_HEADS) % GROUP)[:, None]
    first = FOX_BIAS_OFFSET + 3 * g
    sel = jnp.where((lane >= first) & (lane < first + 3), -1.0, 0.0).astype(F32)
    return w.reshape(d, N_HEADS * LANES).astype(BF16), sel.reshape(1, N_HEADS * LANES)


def _cumsum_kernel(lf_ref, out_ref, carry):
    @pl.when(pl.program_id(1) == 0)
    def _():
        carry[...] = jnp.zeros_like(carry)

    tri = _upper_tri(CUMSUM_BLOCK)
    c = carry[...]
    for part in _split3(lf_ref[...]):
        c = c + _dot(part, tri)
    out_ref[...] = jnp.concatenate(_split3(c), axis=0)
    carry[...] = c[:, CUMSUM_BLOCK - 1:CUMSUM_BLOCK]


def _cumsum_parts(lf_t, batch, t):
    nblk = t // CUMSUM_BLOCK
    return pl.pallas_call(
        _cumsum_kernel, grid=(batch, nblk),
        in_specs=[pl.BlockSpec((N_HEADS, CUMSUM_BLOCK), lambda b, c: (0, b * nblk + c))],
        out_specs=pl.BlockSpec((3 * N_HEADS, CUMSUM_BLOCK), lambda b, c: (0, b * nblk + c)),
        out_shape=jax.ShapeDtypeStruct((3 * N_HEADS, batch * t), BF16),
        scratch_shapes=[pltpu.VMEM((N_HEADS, 1), F32)],
        compiler_params=_params(("parallel", "arbitrary")), name="fox_cumsum")(lf_t)


def _fox_prompt_kernel(q_ref, k_ref, v_ref, o_ref, m_s, acc_s):
    tq, tk = FOX_Q_BLOCK, FOX_K_BLOCK
    rows = GROUP * tq
    i = pl.program_id(1)
    last = (i * tq) // tk
    q_local = lax.broadcasted_iota(jnp.int32, (rows, tk), 0) % tq
    key_local = lax.broadcasted_iota(jnp.int32, (rows, tk), 1)
    causal = key_local - q_local <= i * tq - last * tk

    for pair in range(N_KV // FOX_KV_PER_LOOP):
        kvs = [pair * FOX_KV_PER_LOOP + x for x in range(FOX_KV_PER_LOOP)]
        q_st = [jnp.concatenate(
            [q_ref[:, (kv * GROUP + g) * LANES:(kv * GROUP + g + 1) * LANES] for g in range(GROUP)], axis=0)
            for kv in kvs]
        m_s[...] = jnp.full_like(m_s, -jnp.inf)
        acc_s[...] = jnp.zeros_like(acc_s)

        def step(j, masked):
            start = pl.multiple_of(j * tk, tk)
            keys = pl.ds(start, tk)
            chunks = [(x, kv, slice(c * FOX_ROW_CHUNK, (c + 1) * FOX_ROW_CHUNK))
                      for x, kv in enumerate(kvs) for c in range(rows // FOX_ROW_CHUNK)]

            def scores(x, kv, rs):
                return _dot(q_st[x][rs], k_ref[kv * LANES:(kv + 1) * LANES, keys])

            s_next = scores(*chunks[0])
            for n, (x, kv, rs) in enumerate(chunks):
                s = s_next
                if n + 1 < len(chunks):
                    s_next = scores(*chunks[n + 1])
                if masked:
                    s = jnp.where(causal[rs], s, -jnp.inf)
                m_old = m_s[x, rs]
                m_new = jnp.maximum(m_old, jnp.max(s, axis=-1, keepdims=True))
                p = jnp.exp(s - m_new).astype(BF16)
                m_s[x, rs] = m_new
                acc_s[x, rs] = (jnp.exp(m_old - m_new) * acc_s[x, rs]
                                + _dot_nt(p, v_ref[kv * LANES:(kv + 1) * LANES, keys]))

        def full_step(j, carry):
            step(j, False)
            return carry

        lax.fori_loop(0, last, full_step, 0)
        step(last, True)
        for x, kv in enumerate(kvs):
            acc = acc_s[x]
            out = acc[:, :HEAD_DIM] / acc[:, FOX_ONES_ROW:FOX_ONES_ROW + 1]
            for g in range(GROUP):
                head = kv * GROUP + g
                o_ref[:, head * HEAD_DIM:(head + 1) * HEAD_DIM] = out[g * tq:(g + 1) * tq].astype(BF16)


def _fox_prompt(q_pad, kt_pad, vt_pad, batch, t):
    nq = t // FOX_Q_BLOCK
    seq = pl.BlockSpec((N_KV * LANES, t), lambda b, i: (0, b), pipeline_mode=pl.Buffered(1))
    return pl.pallas_call(
        _fox_prompt_kernel, grid=(batch, nq),
        in_specs=[pl.BlockSpec((FOX_Q_BLOCK, N_HEADS * LANES), lambda b, i: (b * nq + i, 0)), seq, seq],
        out_specs=pl.BlockSpec((FOX_Q_BLOCK, D_MODEL), lambda b, i: (b * nq + i, 0)),
        out_shape=jax.ShapeDtypeStruct((batch * t, D_MODEL), BF16),
        scratch_shapes=[pltpu.VMEM((FOX_KV_PER_LOOP, GROUP * FOX_Q_BLOCK, 1), F32),
                        pltpu.VMEM((FOX_KV_PER_LOOP, GROUP * FOX_Q_BLOCK, LANES), F32)],
        compiler_params=_params(("parallel", "arbitrary")), name="fox_prompt")(q_pad, kt_pad, vt_pad)


def _fox_padded_keys_values(k_t, v_t, c_parts, m_p):
    kt = k_t[:, :m_p].astype(BF16).reshape(N_KV, HEAD_DIM, m_p)
    vt = v_t[:, :m_p].astype(BF16).reshape(N_KV, HEAD_DIM, m_p)
    cp = c_parts.reshape(3, N_KV, GROUP, m_p).transpose(1, 2, 0, 3).reshape(N_KV, 3 * GROUP, m_p)
    k_fill = jnp.zeros((N_KV, LANES - HEAD_DIM - 3 * GROUP, m_p), BF16)
    v_fill = jnp.zeros((N_KV, LANES - HEAD_DIM - 1, m_p), BF16)
    ones = jnp.ones((N_KV, 1, m_p), BF16)
    kt_pad = jnp.concatenate([kt, cp, k_fill], axis=1).reshape(N_KV * LANES, m_p)
    vt_pad = jnp.concatenate([vt, ones, v_fill], axis=1).reshape(N_KV * LANES, m_p)
    return kt_pad, vt_pad


def _lane_cumsum(lf, tri, carry, width, n, repeat):
    heads = lf.shape[0]
    parts = _split3(lf)
    stacked = jnp.concatenate([p[:, i * width:(i + 1) * width] for i in range(n) for p in parts], axis=0)
    cs = _dot(stacked, tri)
    out = []
    for i in range(n):
        base = 3 * heads * i
        g = cs[base:base + heads] + cs[base + heads:base + 2 * heads] + cs[base + 2 * heads:base + 3 * heads]
        g = g + carry
        carry = g[:, width - 1:width]
        out.append(jnp.concatenate([g] * repeat, axis=0))
    return out, carry


def _softmax_step(s, v_t, m_s, l_s, acc_s):
    m_old = m_s[...]
    m_new = jnp.maximum(m_old, jnp.max(s, axis=-1, keepdims=True))
    p = jnp.exp(s - m_new)
    alpha = jnp.exp(m_old - m_new)
    l_s[...] = alpha * l_s[...] + jnp.sum(p, axis=-1, keepdims=True)
    m_s[...] = m_new
    acc_s[...] = alpha * acc_s[...] + _dot_nt(p.astype(BF16), v_t)


def _fox_sample_kernel(pt_ref, q_ref, kn_ref, vn_ref, lfn_ref, *rest, n_new):
    del pt_ref
    pp = PAGES_PER_STEP
    k_refs, v_refs, lf_refs = rest[:pp], rest[pp:2 * pp], rest[2 * pp:3 * pp]
    o_ref, m_s, l_s, acc_s, g_s, kcat, vcat = rest[3 * pp:]
    step = pl.program_id(1)
    rows = N_HEADS * n_new

    @pl.when(step == 0)
    def _():
        m_s[...] = jnp.full_like(m_s, -jnp.inf)
        l_s[...] = jnp.zeros_like(l_s)
        acc_s[...] = jnp.zeros_like(acc_s)
        g_s[...] = jnp.zeros_like(g_s)

    q = q_ref[...]
    for i in range(pp):
        kcat[:, i * PAGE:(i + 1) * PAGE] = k_refs[i][...].astype(BF16)
        vcat[:, i * PAGE:(i + 1) * PAGE] = v_refs[i][...].astype(BF16)
    lf = jnp.concatenate([lf_refs[i][...] for i in range(pp)], axis=1)
    gs, carry = _lane_cumsum(lf, _upper_tri(PAGE), g_s[...], PAGE, pp, n_new)
    g_s[...] = carry
    s = _dot(q, kcat[...]) - jnp.concatenate(gs, axis=1)
    _softmax_step(s, vcat[...], m_s, l_s, acc_s)

    @pl.when(step == pl.num_programs(1) - 1)
    def _():
        gn, _ = _lane_cumsum(lfn_ref[...], _upper_tri(NEW_PAD), carry, NEW_PAD, 1, n_new)
        s_new = _dot(q, kn_ref[...].astype(BF16)) - gn[0]
        u = lax.broadcasted_iota(jnp.int32, (rows, NEW_PAD), 1)
        t = lax.broadcasted_iota(jnp.int32, (rows, NEW_PAD), 0) // N_HEADS
        s_new = jnp.where(u <= t, s_new, -jnp.inf)
        _softmax_step(s_new, vn_ref[...].astype(BF16), m_s, l_s, acc_s)
        o_ref[...] = acc_s[...] / l_s[...]


def _fox_sample(q_bd, kn_t, vn_t, lfn_t, cache_kt, cache_vt, cache_lft, page_ids, n_new):
    nseq, npages = page_ids.shape
    pp = PAGES_PER_STEP
    rows = N_HEADS * n_new
    per_seq = lambda *shape: pl.BlockSpec((None,) + shape, lambda b, s, pt: (b, 0, 0))

    def page(height, i):
        return pl.BlockSpec((None, height, PAGE), lambda b, s, pt: (pt[b, s * pp + i], 0, 0))

    in_specs = ([per_seq(rows, D_KV), per_seq(D_KV, NEW_PAD), per_seq(D_KV, NEW_PAD), per_seq(N_HEADS, NEW_PAD)]
                + [page(D_KV, i) for i in range(pp)] + [page(D_KV, i) for i in range(pp)]
                + [page(N_HEADS, i) for i in range(pp)])
    grid_spec = pltpu.PrefetchScalarGridSpec(
        num_scalar_prefetch=1, grid=(nseq, npages // pp), in_specs=in_specs,
        out_specs=per_seq(rows, D_KV),
        scratch_shapes=[pltpu.VMEM((rows, 1), F32), pltpu.VMEM((rows, 1), F32), pltpu.VMEM((rows, D_KV), F32),
                        pltpu.VMEM((N_HEADS, 1), F32), pltpu.VMEM((D_KV, pp * PAGE), BF16),
                        pltpu.VMEM((D_KV, pp * PAGE), BF16)])
    return pl.pallas_call(
        functools.partial(_fox_sample_kernel, n_new=n_new), grid_spec=grid_spec,
        out_shape=jax.ShapeDtypeStruct((nseq, rows, D_KV), F32),
        compiler_params=_params(("parallel", "arbitrary")), name="fox_sample")(
            page_ids, q_bd, kn_t, vn_t, lfn_t, *([cache_kt] * pp), *([cache_vt] * pp), *([cache_lft] * pp))


def _fox_sample_queries(q_pad, nseq, t):
    q = q_pad.reshape(nseq, t, N_HEADS, LANES)[..., :HEAD_DIM]
    own = (jnp.arange(N_HEADS)[:, None] // GROUP == jnp.arange(N_KV)[None, :]).astype(q.dtype)
    return (q[:, :, :, None, :] * own[None, None, :, :, None]).reshape(nseq, t * N_HEADS, D_KV)


def _fox_sample_outputs(o, nseq, t):
    o6 = o.reshape(nseq, t, N_KV, GROUP, N_KV, HEAD_DIM)
    sel = jnp.stack([o6[:, :, kv, :, kv, :] for kv in range(N_KV)], axis=2)
    return sel.reshape(nseq * t, D_MODEL)


def _new_token_columns(x_t, m_p, nseq, t):
    n = x_t.shape[0]
    x = x_t[:, m_p:].reshape(n, nseq, t).transpose(1, 0, 2)
    return jnp.pad(x, ((0, 0), (0, 0), (0, NEW_PAD - t)))


def _swa_prompt_kernel(q_ref, kp_ref, kc_ref, vp_ref, vc_ref, sink_ref, o_ref):
    w = WINDOW
    i = pl.program_id(1)
    row = lax.broadcasted_iota(jnp.int32, (w, 2 * w), 0)
    col = lax.broadcasted_iota(jnp.int32, (w, 2 * w), 1)
    valid = (col > row) & (col <= row + w) & ((col >= w) | (i > 0))
    for kv in range(N_KV):
        lanes = slice(kv * HEAD_DIM, (kv + 1) * HEAD_DIM)
        q_st = jnp.concatenate(
            [q_ref[:, (kv * GROUP + g) * HEAD_DIM:(kv * GROUP + g + 1) * HEAD_DIM] for g in range(GROUP)],
            axis=0)
        kk = jnp.concatenate([kp_ref[:, lanes], kc_ref[:, lanes]], axis=0)
        vv = jnp.concatenate([vp_ref[:, lanes], vc_ref[:, lanes]], axis=0)
        s = _dot_nt(q_st, kk)
        ps, dens = [], []
        for g in range(GROUP):
            head = kv * GROUP + g
            sg = jnp.where(valid, s[g * w:(g + 1) * w], -jnp.inf)
            sink = sink_ref[0:1, head:head + 1]
            m = jnp.maximum(jnp.max(sg, axis=-1, keepdims=True), sink)
            p = jnp.exp(sg - m)
            dens.append(jnp.sum(p, axis=-1, keepdims=True) + jnp.exp(sink - m))
            ps.append(p.astype(BF16))
        out = _dot(jnp.concatenate(ps, axis=0), vv) / jnp.concatenate(dens, axis=0)
        for g in range(GROUP):
            head = kv * GROUP + g
            o_ref[:, head * HEAD_DIM:(head + 1) * HEAD_DIM] = out[g * w:(g + 1) * w].astype(BF16)


def _swa_prompt(q, kb, vb, sink, batch, t):
    nb = t // WINDOW
    cur = lambda n: pl.BlockSpec((WINDOW, n), lambda b, i: (b * nb + i, 0))
    prev = lambda n: pl.BlockSpec((WINDOW, n), lambda b, i: (b * nb + jnp.maximum(i - 1, 0), 0))
    return pl.pallas_call(
        _swa_prompt_kernel, grid=(batch, nb),
        in_specs=[cur(D_MODEL), prev(D_KV), cur(D_KV), prev(D_KV), cur(D_KV), _resident((1, N_HEADS))],
        out_specs=cur(D_MODEL),
        out_shape=jax.ShapeDtypeStruct((batch * t, D_MODEL), BF16),
        compiler_params=_params(("parallel", "arbitrary")), name="swa_prompt")(q, kb, kb, vb, vb, sink)


def _swa_sample_kernel(q_ref, ck_ref, cv_ref, kn_ref, vn_ref, sink_ref, o_ref, kbuf_ref, vbuf_ref, *, n_new):
    rows = N_HEADS * n_new
    key = lax.broadcasted_iota(jnp.int32, (rows, WINDOW), 1)
    t_c = lax.broadcasted_iota(jnp.int32, (rows, WINDOW), 0) % n_new
    u = lax.broadcasted_iota(jnp.int32, (rows, NEW_PAD), 1)
    t_n = lax.broadcasted_iota(jnp.int32, (rows, NEW_PAD), 0) % n_new
    sink = sink_ref[...]
    for b in range(SWA_SEQS_PER_STEP):
        q = q_ref[b]
        ck, cv, kn, vn = ck_ref[b], cv_ref[b], kn_ref[b], vn_ref[b]
        s_c = jnp.where(key > t_c, _dot_nt(q, ck.astype(BF16)), -jnp.inf)
        s_n = jnp.where(u <= t_n, _dot_nt(q, kn.astype(BF16)), -jnp.inf)
        m = jnp.maximum(jnp.maximum(jnp.max(s_c, axis=-1, keepdims=True),
                                    jnp.max(s_n, axis=-1, keepdims=True)), sink)
        p_c = jnp.exp(s_c - m)
        p_n = jnp.exp(s_n - m)
        den = (jnp.sum(p_c, axis=-1, keepdims=True) + jnp.sum(p_n, axis=-1, keepdims=True)
               + jnp.exp(sink - m))
        o = _dot(p_c.astype(BF16), cv.astype(BF16)) + _dot(p_n.astype(BF16), vn.astype(BF16))
        o_ref[b] = o / den
        kbuf_ref[b, 0:WINDOW - n_new, :] = ck[n_new:, :]
        kbuf_ref[b, WINDOW - n_new:WINDOW, :] = kn[0:n_new, :]
        vbuf_ref[b, 0:WINDOW - n_new, :] = cv[n_new:, :]
        vbuf_ref[b, WINDOW - n_new:WINDOW, :] = vn[0:n_new, :]


def _swa_sample(q_bd, cache_k, cache_v, k_new, v_new, sink_rows, n_new):
    nseq = q_bd.shape[0]
    rows = N_HEADS * n_new
    sb = SWA_SEQS_PER_STEP
    blk = lambda a, b: pl.BlockSpec((sb, a, b), lambda i: (i, 0, 0))
    return pl.pallas_call(
        functools.partial(_swa_sample_kernel, n_new=n_new), grid=(nseq // sb,),
        in_specs=[blk(rows, D_KV), blk(WINDOW, D_KV), blk(WINDOW, D_KV), blk(NEW_PAD, D_KV), blk(NEW_PAD, D_KV),
                  _resident((rows, 1))],
        out_specs=[blk(rows, D_KV), blk(WINDOW, D_KV), blk(WINDOW, D_KV)],
        out_shape=[jax.ShapeDtypeStruct((nseq, rows, D_KV), F32),
                   jax.ShapeDtypeStruct((nseq, WINDOW, D_KV), F32),
                   jax.ShapeDtypeStruct((nseq, WINDOW, D_KV), F32)],
        compiler_params=_params(("parallel",)), name="swa_sample")(
            q_bd, cache_k, cache_v, k_new, v_new, sink_rows)


def _ret_proj_kernel(h_ref, g_ref, w_ref, cos_ref, sin_ref, q_ref, k_ref, v_ref, gate_ref):
    xn = _rms(h_ref[...], g_ref[...]).astype(BF16)
    cos = jnp.concatenate([cos_ref[...]] * RET_HEADS, axis=1)
    sin = jnp.concatenate([sin_ref[...]] * RET_HEADS, axis=1)
    d = D_MODEL
    q_ref[...] = _dot(xn, w_ref[:, 0:d]) * cos + _dot(xn, w_ref[:, d:2 * d]) * sin
    k_scale = RET_DK ** -0.5
    k_ref[...] = (_dot(xn, w_ref[:, 2 * d:3 * d]) * k_scale) * cos + (_dot(xn, w_ref[:, 3 * d:4 * d]) * k_scale) * sin
    v_ref[...] = _dot(xn, w_ref[:, 4 * d:6 * d])
    z = _dot(xn, w_ref[:, 6 * d:8 * d])
    gate_ref[...] = z / (1.0 + jnp.exp(-z))


def _ret_proj(h, g, w, cos_tab, sin_tab, prompt_tiles, tiles_per_seq):
    m = h.shape[0]
    row = lambda n: pl.BlockSpec((ROW_TILE, n), lambda i: (i, 0))
    tab = pl.BlockSpec((ROW_TILE, RET_DK),
                       lambda i: (jnp.where(i < prompt_tiles, i % tiles_per_seq, tiles_per_seq), 0))
    dv = RET_HEADS * RET_DV
    return pl.pallas_call(
        _ret_proj_kernel, grid=(m // ROW_TILE,),
        in_specs=[row(D_MODEL), _resident((1, D_MODEL)), _resident(w.shape), tab, tab],
        out_specs=[row(D_MODEL), row(D_MODEL), row(dv), row(dv)],
        out_shape=[jax.ShapeDtypeStruct((m, D_MODEL), F32), jax.ShapeDtypeStruct((m, D_MODEL), F32),
                   jax.ShapeDtypeStruct((m, dv), F32), jax.ShapeDtypeStruct((m, dv), F32)],
        compiler_params=_params(("parallel",)), name="ret_proj")(h, g, w, cos_tab, sin_tab)


def _ret_chunk_kernel(*refs, has_init):
    if has_init:
        q_ref, k_ref, v_ref, gate_ref, gn_ref, decay_ref, xi_ref, zeta_ref, gl_ref, r0_ref, y_ref, r_ref, r_s = refs
    else:
        q_ref, k_ref, v_ref, gate_ref, gn_ref, decay_ref, xi_ref, zeta_ref, gl_ref, y_ref, r_ref, r_s = refs

    @pl.when(pl.program_id(2) == 0)
    def _():
        r_s[...] = r0_ref[...] if has_init else jnp.zeros_like(r_s)

    q = q_ref[...]
    k = k_ref[...]
    vb = v_ref[...].astype(BF16)
    r = r_s[...]
    s = _dot_nt(q.astype(BF16), k.astype(BF16)) * decay_ref[...]
    o = _dot(s.astype(BF16), vb) + _dot((q * xi_ref[...]).astype(BF16), r.astype(BF16))
    kz = (k * zeta_ref[...]).astype(BF16)
    kz_t = _dot_nt(_eye(RET_DK, BF16), kz).astype(BF16)
    r_new = r * gl_ref[...] + _dot(kz_t, vb)
    r_s[...] = r_new
    r_ref[...] = r_new

    mu = jnp.mean(o, axis=-1, keepdims=True)
    oc = o - mu
    var = jnp.mean(oc * oc, axis=-1, keepdims=True)
    y = oc * lax.rsqrt(var + EPS) * gn_ref[...]
    y_ref[...] = (gate_ref[...] * y).astype(BF16)


def _ret_chunks(q, k, v, gate, gn, tables, batch, nchunk, chunk, r0=None):
    decay, xi, zeta, gl = tables
    has_init = r0 is not None
    tok = lambda n: pl.BlockSpec((chunk, n), lambda b, h, c: (b * nchunk + c, h))
    per_head = lambda a, b_: pl.BlockSpec((None, a, b_), lambda b, h, c: (h, 0, 0))
    state = pl.BlockSpec((None, None, RET_DK, RET_DV), lambda b, h, c: (b, h, 0, 0))
    in_specs = [tok(RET_DK), tok(RET_DK), tok(RET_DV), tok(RET_DV),
                pl.BlockSpec((1, RET_DV), lambda b, h, c: (0, h)),
                per_head(chunk, chunk), per_head(chunk, 1), per_head(chunk, 1), per_head(1, 1)]
    args = [q, k, v, gate, gn, decay, xi, zeta, gl]
    if has_init:
        in_specs.append(state)
        args.append(r0)
    return pl.pallas_call(
        functools.partial(_ret_chunk_kernel, has_init=has_init), grid=(batch, RET_HEADS, nchunk),
        in_specs=in_specs, out_specs=[tok(RET_DV), state],
        out_shape=[jax.ShapeDtypeStruct((batch * nchunk * chunk, RET_HEADS * RET_DV), BF16),
                   jax.ShapeDtypeStruct((batch, RET_HEADS, RET_DK, RET_DV), F32)],
        scratch_shapes=[pltpu.VMEM((RET_DK, RET_DV), F32)],
        compiler_params=_params(("parallel", "parallel", "arbitrary")), name="ret_chunks")(*args)


def _ret_tables(chunk, n_valid):
    lg = jnp.log(1.0 - 2.0 ** (-5.0 - jnp.arange(RET_HEADS, dtype=F32)))
    idx = jnp.arange(chunk, dtype=F32)
    rel = idx[:, None] - idx[None, :]
    decay = jnp.where(rel >= 0, jnp.exp(lg[:, None, None] * jnp.maximum(rel, 0.0)), 0.0)
    xi = jnp.exp(lg[:, None] * (idx[None, :] + 1.0))[:, :, None]
    zeta = jnp.where(idx[None, :] < n_valid, jnp.exp(lg[:, None] * (n_valid - 1.0 - idx[None, :])), 0.0)[:, :, None]
    gl = jnp.exp(lg * n_valid)[:, None, None]
    return decay, xi, zeta, gl


def _rotation_tables(pos):
    freqs = 1.0 / (RET_THETA_BASE ** jnp.linspace(0.0, 1.0, RET_DK // 2, dtype=F32))
    ang = pos.astype(F32)[:, None] * freqs[None, :]
    cos = jnp.repeat(jnp.cos(ang), 2, axis=1)
    sign = jnp.tile(jnp.array([-1.0, 1.0], F32), RET_DK // 2)
    sin = jnp.repeat(jnp.sin(ang), 2, axis=1) * sign[None, :]
    return cos, sin


def _swap_pairs(w):
    d_in, d_out = w.shape
    return w.reshape(d_in, d_out // 2, 2)[:, :, ::-1].reshape(d_in, d_out)


def _post_kernel(o_ref, h_ref, p_ref, wo_ref, gm_ref, up_ref, dn_ref, gp_ref, wg_ref, wp_ref, gf_ref, out_ref,
                 *, final):
    h = h_ref[...] + _dot(o_ref[...], wo_ref[...])
    xn = _rms(h, gm_ref[...]).astype(BF16)
    mlp = jnp.zeros_like(h)
    for c in range(D_FF // D_MODEL):
        cols = slice(c * D_MODEL, (c + 1) * D_MODEL)
        u = jnp.maximum(_dot(xn, up_ref[:, cols]), 0.0)
        mlp = mlp + _dot((u * u).astype(BF16), dn_ref[cols, :])
    h = h + mlp
    xn = _rms(h, gp_ref[...]).astype(BF16)
    gate = 1.0 / (1.0 + jnp.exp(-_dot(xn, wg_ref[...])))
    h = h + gate * _dot(p_ref[...].astype(BF16), wp_ref[...])
    if final:
        h = _rms(h, gf_ref[...])
    out_ref[...] = h


def _post(o, h, p, wo, gm, up, dn, gp, wg, wp, gf, final):
    m = h.shape[0]
    row = lambda n: pl.BlockSpec((ROW_TILE, n), lambda i: (i, 0))
    vec = _resident((1, D_MODEL))
    return pl.pallas_call(
        functools.partial(_post_kernel, final=final), grid=(m // ROW_TILE,),
        in_specs=[row(o.shape[1]), row(D_MODEL), row(D_PLE), _resident(wo.shape), vec, _resident(up.shape),
                  _resident(dn.shape), vec, _resident(wg.shape), _resident(wp.shape), vec],
        out_specs=row(D_MODEL), out_shape=jax.ShapeDtypeStruct((m, D_MODEL), F32),
        compiler_params=_params(("parallel",)), name="post_mixer")(o, h, p, wo, gm, up, dn, gp, wg, wp, gf)


def _block_diag_queries(q, nseq, t):
    q5 = q.reshape(nseq, t, N_KV, GROUP, HEAD_DIM).transpose(0, 2, 3, 1, 4)
    q4 = q5.reshape(nseq, N_KV, GROUP * t, 1, HEAD_DIM)
    eye = jnp.eye(N_KV, dtype=q.dtype)[None, :, None, :, None]
    return (q4 * eye).reshape(nseq, N_KV * GROUP * t, D_KV)


def _block_diag_outputs(o, nseq, t):
    o5 = o.reshape(nseq, N_KV, GROUP * t, N_KV, HEAD_DIM)
    sel = jnp.stack([o5[:, kv, :, kv, :] for kv in range(N_KV)], axis=1)
    return sel.reshape(nseq, N_KV, GROUP, t, HEAD_DIM).transpose(0, 3, 1, 2, 4).reshape(nseq * t, D_MODEL)


def _pad_tokens(x, nseq, t, width):
    return jnp.pad(x.reshape(nseq, t, width), ((0, 0), (0, NEW_PAD - t), (0, 0)))


def kernel(x_prompt, x_sample, cache_fox_k, cache_fox_v, cache_fox_logf, cache_swa_k, cache_swa_v, state_ret, page_table, p_prompt, p_sample, fox_wq, fox_wk, fox_wv, fox_wf, fox_bf, fox_wo, swa_wq, swa_wk, swa_wv, swa_sink, swa_wo, ret_wq, ret_wk, ret_wv, ret_wg, ret_gn, ret_wo, norm_mix, norm_mlp, mlp_up, mlp_down, norm_ple, ple_gate, ple_proj, norm_final):
    batch, t_p, d = x_prompt.shape
    nseq, t_s, _ = x_sample.shape
    depth = norm_mix.shape[0]
    m_p = batch * t_p
    m_s = nseq * t_s
    n_pool = cache_fox_k.shape[1]
    past = page_table.shape[1] * PAGE
    assert d == D_MODEL and (m_p + m_s) % ROW_TILE == 0 and m_p % ROW_TILE == 0 and m_s == ROW_TILE
    assert t_p % ROW_TILE == 0 and t_s <= NEW_PAD and N_HEADS * t_s == 64

    h = jnp.concatenate([x_prompt.reshape(m_p, d), x_sample.reshape(m_s, d)], axis=0)
    p_all = jnp.concatenate([p_prompt.reshape(depth, m_p, D_PLE), p_sample.reshape(depth, m_s, D_PLE)], axis=1)
    vec = lambda a: a.reshape(1, -1)

    fox_cache_kt = cache_fox_k.transpose(0, 1, 3, 4, 2).reshape(-1, D_KV, PAGE)
    fox_cache_vt = cache_fox_v.transpose(0, 1, 3, 4, 2).reshape(-1, D_KV, PAGE)
    fox_cache_lft = cache_fox_logf.transpose(0, 1, 3, 2).reshape(-1, N_HEADS, PAGE)

    outs = {name: [] for name in ("fkp", "fvp", "flp", "fks", "fvs", "fls", "skp", "svp", "sks", "svs", "rsp", "rss")}

    for i in range(depth):
        kind, j = i % N_MIXERS, i // N_MIXERS
        g_mix = vec(norm_mix[i])
        if kind == 0:
            wq_pad, sel = _fox_query_weights(fox_wq[j])
            wf_t = jnp.pad(fox_wf[j].T, ((0, LANES - N_HEADS), (0, 0)))
            w_t = jnp.concatenate([fox_wk[j].T, fox_wv[j].T, wf_t], axis=0).astype(BF16)
            q_pad, k_t, v_t, lf_t = _fox_proj(h, g_mix, wq_pad, sel, w_t, fox_bf[j].reshape(N_HEADS, 1))
            c_parts = _cumsum_parts(lf_t, batch, t_p)
            kt_pad, vt_pad = _fox_padded_keys_values(k_t, v_t, c_parts, m_p)
            o_p = _fox_prompt(q_pad, kt_pad, vt_pad, batch, t_p)
            o_bd = _fox_sample(_fox_sample_queries(q_pad[m_p:], nseq, t_s),
                               _new_token_columns(k_t, m_p, nseq, t_s), _new_token_columns(v_t, m_p, nseq, t_s),
                               _new_token_columns(lf_t, m_p, nseq, t_s),
                               fox_cache_kt, fox_cache_vt, fox_cache_lft, page_table + j * n_pool, t_s)
            o_s = _fox_sample_outputs(o_bd, nseq, t_s).astype(BF16)
            o = jnp.concatenate([o_p, o_s], axis=0)
            wo = fox_wo[j]
            prompt_rows = lambda x_t, n: x_t[:, :m_p].reshape(n, batch, t_p).transpose(1, 2, 0)
            sample_rows = lambda x_t, n: x_t[:, m_p:].reshape(n, nseq, t_s).transpose(1, 2, 0)
            outs["fkp"].append(prompt_rows(k_t, D_KV).reshape(batch, t_p, N_KV, HEAD_DIM))
            outs["fvp"].append(prompt_rows(v_t, D_KV).reshape(batch, t_p, N_KV, HEAD_DIM))
            outs["flp"].append(prompt_rows(lf_t, N_HEADS))
            outs["fks"].append(sample_rows(k_t, D_KV).reshape(nseq, t_s, N_KV, HEAD_DIM))
            outs["fvs"].append(sample_rows(v_t, D_KV).reshape(nseq, t_s, N_KV, HEAD_DIM))
            outs["fls"].append(sample_rows(lf_t, N_HEADS))
        elif kind == 1:
            w = jnp.concatenate([swa_wq[j], swa_wk[j], swa_wv[j]], axis=1).astype(BF16)
            q, k, v, kb, vb = _swa_proj(h, g_mix, w)
            o_p = _swa_prompt(q, kb, vb, vec(swa_sink[j]), batch, t_p)
            q_bd = _block_diag_queries(q[m_p:], nseq, t_s)
            sink_rows = jnp.repeat(swa_sink[j], t_s).reshape(N_HEADS * t_s, 1)
            win = cache_swa_k.shape[2]
            o_bd, kbuf, vbuf = _swa_sample(
                q_bd, cache_swa_k[j].reshape(nseq, win, D_KV), cache_swa_v[j].reshape(nseq, win, D_KV),
                _pad_tokens(k[m_p:], nseq, t_s, D_KV), _pad_tokens(v[m_p:], nseq, t_s, D_KV), sink_rows, t_s)
            o_s = _block_diag_outputs(o_bd, nseq, t_s).astype(BF16)
            o = jnp.concatenate([o_p, o_s], axis=0)
            wo = swa_wo[j]
            wp = min(WINDOW, t_p)
            outs["skp"].append(k[:m_p].reshape(batch, t_p, N_KV, HEAD_DIM)[:, -wp:])
            outs["svp"].append(v[:m_p].reshape(batch, t_p, N_KV, HEAD_DIM)[:, -wp:])
            outs["sks"].append(kbuf.reshape(nseq, win, N_KV, HEAD_DIM))
            outs["svs"].append(vbuf.reshape(nseq, win, N_KV, HEAD_DIM))
        else:
            w = jnp.concatenate([ret_wq[j], _swap_pairs(ret_wq[j]), ret_wk[j], _swap_pairs(ret_wk[j]),
                                 ret_wv[j], ret_wg[j]], axis=1).astype(BF16)
            cos_p, sin_p = _rotation_tables(jnp.arange(t_p))
            cos_s, sin_s = _rotation_tables(past + jnp.arange(t_s))
            cos_tab = jnp.concatenate([cos_p, jnp.tile(cos_s, (nseq, 1))], axis=0)
            sin_tab = jnp.concatenate([sin_p, jnp.tile(sin_s, (nseq, 1))], axis=0)
            qr, kr, v, gate = _ret_proj(h, g_mix, w, cos_tab, sin_tab, m_p // ROW_TILE, t_p // ROW_TILE)
            gn = vec(ret_gn[j])
            y_p, r_p = _ret_chunks(qr, kr, v, gate, gn, _ret_tables(RET_CHUNK, RET_CHUNK),
                                   batch, t_p // RET_CHUNK, RET_CHUNK)
            dv = RET_HEADS * RET_DV
            pad = lambda a, n: _pad_tokens(a[m_p:], nseq, t_s, n).reshape(nseq * NEW_PAD, n)
            y_s, r_s = _ret_chunks(pad(qr, D_MODEL), pad(kr, D_MODEL), pad(v, dv), pad(gate, dv), gn,
                                   _ret_tables(NEW_PAD, t_s), nseq, 1, NEW_PAD,
                                   r0=state_ret[j].astype(F32))
            y_s = y_s.reshape(nseq, NEW_PAD, dv)[:, :t_s].reshape(m_s, dv)
            o = jnp.concatenate([y_p, y_s], axis=0)
            wo = ret_wo[j]
            outs["rsp"].append(r_p)
            outs["rss"].append(r_s)
        h = _post(o, h, p_all[i], wo.astype(BF16), vec(norm_mlp[i]), mlp_up[i].astype(BF16),
                  mlp_down[i].astype(BF16), vec(norm_ple[i]), ple_gate[i].astype(BF16),
                  ple_proj[i].astype(BF16), vec(norm_final), final=(i == depth - 1))

    y_prompt = h[:m_p].reshape(batch, t_p, d)
    y_sample = h[m_p:].reshape(nseq, t_s, d)
    st = lambda name: jnp.stack(outs[name])
    return (y_prompt, y_sample, st("fkp"), st("fvp"), st("flp"), st("fks"), st("fvs"), st("fls"),
            st("skp"), st("svp"), st("sks"), st("svs"), st("rsp"), st("rss"))
```

```python
import functools

import jax
import jax.numpy as jnp
from jax import lax
from jax.experimental import pallas as pl
from jax.experimental.pallas import tpu as pltpu

F32 = jnp.float32
BF16 = jnp.bfloat16

D_MODEL = 1024
N_HEADS = 16
N_KV = 4
GROUP = 4
HEAD_DIM = 64
D_KV = N_KV * HEAD_DIM
ATTN_SCALE = HEAD_DIM ** -0.5
WINDOW = 128
PAGE = 128
RET_HEADS = 4
RET_DK = 256
RET_DV = 512
RET_CHUNK = 128
RET_THETA_BASE = 10000.0
D_FF = 4096
D_PLE = 256
EPS = 1e-6
N_MIXERS = 3

LANES = 128
ROW_TILE = 512
FOX_Q_BLOCK = 256
FOX_K_BLOCK = 1024
FOX_KV_PER_LOOP = 2
FOX_ROW_CHUNK = 256
CUMSUM_BLOCK = 512
PAGES_PER_STEP = 16
NEW_PAD = 16
SWA_SEQS_PER_STEP = 8
VMEM_LIMIT = 56 * 1024 * 1024

FOX_BIAS_OFFSET = HEAD_DIM
FOX_ONES_ROW = HEAD_DIM

NT_DIMS = (((1,), (1,)), ((), ()))


def _params(semantics):
    return pltpu.CompilerParams(dimension_semantics=semantics, vmem_limit_bytes=VMEM_LIMIT)


def _resident(shape):
    zeros = (0,) * len(shape)
    return pl.BlockSpec(shape, lambda *_: zeros, pipeline_mode=pl.Buffered(1))


def _rms(x, g):
    ms = jnp.mean(x * x, axis=-1, keepdims=True)
    return x * lax.rsqrt(ms + EPS) * g


def _dot(a, b):
    return jnp.dot(a, b, preferred_element_type=F32)


def _dot_nt(a, b):
    return lax.dot_general(a, b, NT_DIMS, preferred_element_type=F32)


def _split3(x):
    hi = x.astype(BF16)
    r = x - hi.astype(F32)
    mid = r.astype(BF16)
    lo = (r - mid.astype(F32)).astype(BF16)
    return hi, mid, lo


def _eye(n, dtype):
    r = lax.broadcasted_iota(jnp.int32, (n, n), 0)
    c = lax.broadcasted_iota(jnp.int32, (n, n), 1)
    return (r == c).astype(dtype)


def _upper_tri(n):
    r = lax.broadcasted_iota(jnp.int32, (n, n), 0)
    c = lax.broadcasted_iota(jnp.int32, (n, n), 1)
    return (r <= c).astype(BF16)


def _log_sigmoid(z):
    return jnp.minimum(z, 0.0) - jnp.log1p(jnp.exp(-jnp.abs(z)))


def _swa_proj_kernel(h_ref, g_ref, w_ref, q_ref, k_ref, v_ref, kb_ref, vb_ref):
    xn = _rms(h_ref[...], g_ref[...]).astype(BF16)
    y = _dot(xn, w_ref[...])
    q_ref[...] = (y[:, :D_MODEL] * ATTN_SCALE).astype(BF16)
    k = y[:, D_MODEL:D_MODEL + D_KV]
    v = y[:, D_MODEL + D_KV:D_MODEL + 2 * D_KV]
    k_ref[...] = k
    v_ref[...] = v
    kb_ref[...] = k.astype(BF16)
    vb_ref[...] = v.astype(BF16)


def _swa_proj(h, g, w):
    m = h.shape[0]
    row = lambda n: pl.BlockSpec((ROW_TILE, n), lambda i: (i, 0))
    return pl.pallas_call(
        _swa_proj_kernel, grid=(m // ROW_TILE,),
        in_specs=[row(D_MODEL), _resident((1, D_MODEL)), _resident(w.shape)],
        out_specs=[row(D_MODEL), row(D_KV), row(D_KV), row(D_KV), row(D_KV)],
        out_shape=[jax.ShapeDtypeStruct((m, D_MODEL), BF16),
                   jax.ShapeDtypeStruct((m, D_KV), F32), jax.ShapeDtypeStruct((m, D_KV), F32),
                   jax.ShapeDtypeStruct((m, D_KV), BF16), jax.ShapeDtypeStruct((m, D_KV), BF16)],
        compiler_params=_params(("parallel",)), name="swa_proj")(h, g, w)


def _fox_proj_kernel(h_ref, g_ref, wq_ref, sel_ref, wt_ref, bf_ref, q_ref, kt_ref, vt_ref, lft_ref):
    xn = _rms(h_ref[...], g_ref[...]).astype(BF16)
    q_ref[...] = (_dot(xn, wq_ref[...]) * ATTN_SCALE + sel_ref[...]).astype(BF16)
    t = _dot_nt(wt_ref[...], xn)
    kt_ref[...] = t[0:D_KV]
    vt_ref[...] = t[D_KV:2 * D_KV]
    lft_ref[...] = _log_sigmoid(t[2 * D_KV:2 * D_KV + N_HEADS] + bf_ref[...])


def _fox_proj(h, g, wq_pad, sel, w_t, bf_col):
    m = h.shape[0]
    row = lambda n: pl.BlockSpec((ROW_TILE, n), lambda i: (i, 0))
    col = lambda n: pl.BlockSpec((n, ROW_TILE), lambda i: (0, i))
    return pl.pallas_call(
        _fox_proj_kernel, grid=(m // ROW_TILE,),
        in_specs=[row(D_MODEL), _resident((1, D_MODEL)), _resident(wq_pad.shape), _resident(sel.shape),
                  _resident(w_t.shape), _resident(bf_col.shape)],
        out_specs=[row(N_HEADS * LANES), col(D_KV), col(D_KV), col(N_HEADS)],
        out_shape=[jax.ShapeDtypeStruct((m, N_HEADS * LANES), BF16),
                   jax.ShapeDtypeStruct((D_KV, m), F32), jax.ShapeDtypeStruct((D_KV, m), F32),
                   jax.ShapeDtypeStruct((N_HEADS, m), F32)],
        compiler_params=_params(("parallel",)), name="fox_proj")(h, g, wq_pad, sel, w_t, bf_col)


def _fox_query_weights(wq):
    d = wq.shape[0]
    w = jnp.pad(wq.reshape(d, N_HEADS, HEAD_DIM), ((0, 0), (0, 0), (0, LANES - HEAD_DIM)))
    lane = jnp.arange(LANES)[None, :]
    g = (jnp.arange(N_HEADS) % GROUP)[:, None]
    first = FOX_BIAS_OFFSET + 3 * g
    sel = jnp.where((lane >= first) & (lane < first + 3), -1.0, 0.0).astype(F32)
    return w.reshape(d, N_HEADS * LANES).astype(BF16), sel.reshape(1, N_HEADS * LANES)


def _cumsum_kernel(lf_ref, out_ref, carry):
    @pl.when(pl.program_id(1) == 0)
    def _():
        carry[...] = jnp.zeros_like(carry)

    tri = _upper_tri(CUMSUM_BLOCK)
    c = carry[...]
    for part in _split3(lf_ref[...]):
        c = c + _dot(part, tri)
    out_ref[...] = jnp.concatenate(_split3(c), axis=0)
    carry[...] = c[:, CUMSUM_BLOCK - 1:CUMSUM_BLOCK]


def _cumsum_parts(lf_t, batch, t):
    nblk = t // CUMSUM_BLOCK
    return pl.pallas_call(
        _cumsum_kernel, grid=(batch, nblk),
        in_specs=[pl.BlockSpec((N_HEADS, CUMSUM_BLOCK), lambda b, c: (0, b * nblk + c))],
        out_specs=pl.BlockSpec((3 * N_HEADS, CUMSUM_BLOCK), lambda b, c: (0, b * nblk + c)),
        out_shape=jax.ShapeDtypeStruct((3 * N_HEADS, batch * t), BF16),
        scratch_shapes=[pltpu.VMEM((N_HEADS, 1), F32)],
        compiler_params=_params(("parallel", "arbitrary")), name="fox_cumsum")(lf_t)


def _fox_prompt_kernel(q_ref, k_ref, v_ref, o_ref, m_s, acc_s):
    tq, tk = FOX_Q_BLOCK, FOX_K_BLOCK
    rows = GROUP * tq
    i = pl.program_id(1)
    last = (i * tq) // tk
    q_local = lax.broadcasted_iota(jnp.int32, (rows, tk), 0) % tq
    key_local = lax.broadcasted_iota(jnp.int32, (rows, tk), 1)
    causal = key_local - q_local <= i * tq - last * tk

    for pair in range(N_KV // FOX_KV_PER_LOOP):
        kvs = [pair * FOX_KV_PER_LOOP + x for x in range(FOX_KV_PER_LOOP)]
        q_st = [jnp.concatenate(
            [q_ref[:, (kv * GROUP + g) * LANES:(kv * GROUP + g + 1) * LANES] for g in range(GROUP)], axis=0)
            for kv in kvs]
        m_s[...] = jnp.full_like(m_s, -jnp.inf)
        acc_s[...] = jnp.zeros_like(acc_s)

        def step(j, masked):
            start = pl.multiple_of(j * tk, tk)
            keys = pl.ds(start, tk)
            chunks = [(x, kv, slice(c * FOX_ROW_CHUNK, (c + 1) * FOX_ROW_CHUNK))
                      for x, kv in enumerate(kvs) for c in range(rows // FOX_ROW_CHUNK)]

            def scores(x, kv, rs):
                return _dot(q_st[x][rs], k_ref[kv * LANES:(kv + 1) * LANES, keys])

            s_next = scores(*chunks[0])
            for n, (x, kv, rs) in enumerate(chunks):
                s = s_next
                if n + 1 < len(chunks):
                    s_next = scores(*chunks[n + 1])
                if masked:
                    s = jnp.where(causal[rs], s, -jnp.inf)
                m_old = m_s[x, rs]
                m_new = jnp.maximum(m_old, jnp.max(s, axis=-1, keepdims=True))
                p = jnp.exp(s - m_new).astype(BF16)
                m_s[x, rs] = m_new
                acc_s[x, rs] = (jnp.exp(m_old - m_new) * acc_s[x, rs]
                                + _dot_nt(p, v_ref[kv * LANES:(kv + 1) * LANES, keys]))

        def full_step(j, carry):
            step(j, False)
            return carry

        lax.fori_loop(0, last, full_step, 0)
        step(last, True)
        for x, kv in enumerate(kvs):
            acc = acc_s[x]
            out = acc[:, :HEAD_DIM] / acc[:, FOX_ONES_ROW:FOX_ONES_ROW + 1]
            for g in range(GROUP):
                head = kv * GROUP + g
                o_ref[:, head * HEAD_DIM:(head + 1) * HEAD_DIM] = out[g * tq:(g + 1) * tq].astype(BF16)


def _fox_prompt(q_pad, kt_pad, vt_pad, batch, t):
    nq = t // FOX_Q_BLOCK
    seq = pl.BlockSpec((N_KV * LANES, t), lambda b, i: (0, b), pipeline_mode=pl.Buffered(1))
    return pl.pallas_call(
        _fox_prompt_kernel, grid=(batch, nq),
        in_specs=[pl.BlockSpec((FOX_Q_BLOCK, N_HEADS * LANES), lambda b, i: (b * nq + i, 0)), seq, seq],
        out_specs=pl.BlockSpec((FOX_Q_BLOCK, D_MODEL), lambda b, i: (b * nq + i, 0)),
        out_shape=jax.ShapeDtypeStruct((batch * t, D_MODEL), BF16),
        scratch_shapes=[pltpu.VMEM((FOX_KV_PER_LOOP, GROUP * FOX_Q_BLOCK, 1), F32),
                        pltpu.VMEM((FOX_KV_PER_LOOP, GROUP * FOX_Q_BLOCK, LANES), F32)],
        compiler_params=_params(("parallel", "arbitrary")), name="fox_prompt")(q_pad, kt_pad, vt_pad)


def _fox_padded_keys_values(k_t, v_t, c_parts, m_p):
    kt = k_t[:, :m_p].astype(BF16).reshape(N_KV, HEAD_DIM, m_p)
    vt = v_t[:, :m_p].astype(BF16).reshape(N_KV, HEAD_DIM, m_p)
    cp = c_parts.reshape(3, N_KV, GROUP, m_p).transpose(1, 2, 0, 3).reshape(N_KV, 3 * GROUP, m_p)
    k_fill = jnp.zeros((N_KV, LANES - HEAD_DIM - 3 * GROUP, m_p), BF16)
    v_fill = jnp.zeros((N_KV, LANES - HEAD_DIM - 1, m_p), BF16)
    ones = jnp.ones((N_KV, 1, m_p), BF16)
    kt_pad = jnp.concatenate([kt, cp, k_fill], axis=1).reshape(N_KV * LANES, m_p)
    vt_pad = jnp.concatenate([vt, ones, v_fill], axis=1).reshape(N_KV * LANES, m_p)
    return kt_pad, vt_pad


def _lane_cumsum(lf, tri, carry, width, n, repeat):
    heads = lf.shape[0]
    parts = _split3(lf)
    stacked = jnp.concatenate([p[:, i * width:(i + 1) * width] for i in range(n) for p in parts], axis=0)
    cs = _dot(stacked, tri)
    key_chunk = lax.broadcasted_iota(jnp.int32, (n * width, LANES), 0) // width
    col = lax.broadcasted_iota(jnp.int32, (n * width, LANES), 1)
    before = (key_chunk < col).astype(BF16)
    offsets = carry + (_dot(parts[0], before) + _dot(parts[1], before) + _dot(parts[2], before))
    out = []
    for i in range(n):
        base = 3 * heads * i
        g = cs[base:base + heads] + cs[base + heads:base + 2 * heads] + cs[base + 2 * heads:base + 3 * heads]
        g = g + offsets[:, i:i + 1]
        out.append(jnp.concatenate([g] * repeat, axis=0))
    return out, offsets[:, n:n + 1]


def _softmax_step(s, v_t, m_s, l_s, acc_s):
    m_old = m_s[...]
    m_new = jnp.maximum(m_old, jnp.max(s, axis=-1, keepdims=True))
    p = jnp.exp(s - m_new)
    alpha = jnp.exp(m_old - m_new)
    l_s[...] = alpha * l_s[...] + jnp.sum(p, axis=-1, keepdims=True)
    m_s[...] = m_new
    acc_s[...] = alpha * acc_s[...] + _dot_nt(p.astype(BF16), v_t)


def _fox_sample_kernel(pt_ref, q_ref, kn_ref, vn_ref, lfn_ref, *rest, n_new):
    del pt_ref
    pp = PAGES_PER_STEP
    k_refs, v_refs, lf_refs = rest[:pp], rest[pp:2 * pp], rest[2 * pp:3 * pp]
    o_ref, m_s, l_s, acc_s, g_s, kcat, vcat = rest[3 * pp:]
    step = pl.program_id(1)
    rows = N_HEADS * n_new

    @pl.when(step == 0)
    def _():
        m_s[...] = jnp.full_like(m_s, -jnp.inf)
        l_s[...] = jnp.zeros_like(l_s)
        acc_s[...] = jnp.zeros_like(acc_s)
        g_s[...] = jnp.zeros_like(g_s)

    q = q_ref[...]
    for i in range(pp):
        kcat[:, i * PAGE:(i + 1) * PAGE] = k_refs[i][...].astype(BF16)
        vcat[:, i * PAGE:(i + 1) * PAGE] = v_refs[i][...].astype(BF16)
    lf = jnp.concatenate([lf_refs[i][...] for i in range(pp)], axis=1)
    gs, carry = _lane_cumsum(lf, _upper_tri(PAGE), g_s[...], PAGE, pp, n_new)
    g_s[...] = carry
    s = _dot(q, kcat[...]) - jnp.concatenate(gs, axis=1)
    _softmax_step(s, vcat[...], m_s, l_s, acc_s)

    @pl.when(step == pl.num_programs(1) - 1)
    def _():
        gn, _ = _lane_cumsum(lfn_ref[...], _upper_tri(NEW_PAD), carry, NEW_PAD, 1, n_new)
        s_new = _dot(q, kn_ref[...].astype(BF16)) - gn[0]
        u = lax.broadcasted_iota(jnp.int32, (rows, NEW_PAD), 1)
        t = lax.broadcasted_iota(jnp.int32, (rows, NEW_PAD), 0) // N_HEADS
        s_new = jnp.where(u <= t, s_new, -jnp.inf)
        _softmax_step(s_new, vn_ref[...].astype(BF16), m_s, l_s, acc_s)
        o_ref[...] = acc_s[...] / l_s[...]


def _fox_sample(q_bd, kn_t, vn_t, lfn_t, cache_kt, cache_vt, cache_lft, page_ids, n_new):
    nseq, npages = page_ids.shape
    pp = PAGES_PER_STEP
    rows = N_HEADS * n_new
    per_seq = lambda *shape: pl.BlockSpec((None,) + shape, lambda b, s, pt: (b, 0, 0))

    def page(height, i):
        return pl.BlockSpec((None, height, PAGE), lambda b, s, pt: (pt[b, s * pp + i], 0, 0))

    in_specs = ([per_seq(rows, D_KV), per_seq(D_KV, NEW_PAD), per_seq(D_KV, NEW_PAD), per_seq(N_HEADS, NEW_PAD)]
                + [page(D_KV, i) for i in range(pp)] + [page(D_KV, i) for i in range(pp)]
                + [page(N_HEADS, i) for i in range(pp)])
    grid_spec = pltpu.PrefetchScalarGridSpec(
        num_scalar_prefetch=1, grid=(nseq, npages // pp), in_specs=in_specs,
        out_specs=per_seq(rows, D_KV),
        scratch_shapes=[pltpu.VMEM((rows, 1), F32), pltpu.VMEM((rows, 1), F32), pltpu.VMEM((rows, D_KV), F32),
                        pltpu.VMEM((N_HEADS, 1), F32), pltpu.VMEM((D_KV, pp * PAGE), BF16),
                        pltpu.VMEM((D_KV, pp * PAGE), BF16)])
    return pl.pallas_call(
        functools.partial(_fox_sample_kernel, n_new=n_new), grid_spec=grid_spec,
        out_shape=jax.ShapeDtypeStruct((nseq, rows, D_KV), F32),
        compiler_params=_params(("parallel", "arbitrary")), name="fox_sample")(
            page_ids, q_bd, kn_t, vn_t, lfn_t, *([cache_kt] * pp), *([cache_vt] * pp), *([cache_lft] * pp))


def _fox_sample_queries(q_pad, nseq, t):
    q = q_pad.reshape(nseq, t, N_HEADS, LANES)[..., :HEAD_DIM]
    own = (jnp.arange(N_HEADS)[:, None] // GROUP == jnp.arange(N_KV)[None, :]).astype(q.dtype)
    return (q[:, :, :, None, :] * own[None, None, :, :, None]).reshape(nseq, t * N_HEADS, D_KV)


def _fox_sample_outputs(o, nseq, t):
    o6 = o.reshape(nseq, t, N_KV, GROUP, N_KV, HEAD_DIM)
    sel = jnp.stack([o6[:, :, kv, :, kv, :] for kv in range(N_KV)], axis=2)
    return sel.reshape(nseq * t, D_MODEL)


def _new_token_columns(x_t, m_p, nseq, t):
    n = x_t.shape[0]
    x = x_t[:, m_p:].reshape(n, nseq, t).transpose(1, 0, 2)
    return jnp.pad(x, ((0, 0), (0, 0), (0, NEW_PAD - t)))


def _swa_prompt_kernel(q_ref, kp_ref, kc_ref, vp_ref, vc_ref, sink_ref, o_ref):
    w = WINDOW
    i = pl.program_id(1)
    row = lax.broadcasted_iota(jnp.int32, (w, 2 * w), 0)
    col = lax.broadcasted_iota(jnp.int32, (w, 2 * w), 1)
    valid = (col > row) & (col <= row + w) & ((col >= w) | (i > 0))
    for kv in range(N_KV):
        lanes = slice(kv * HEAD_DIM, (kv + 1) * HEAD_DIM)
        q_st = jnp.concatenate(
            [q_ref[:, (kv * GROUP + g) * HEAD_DIM:(kv * GROUP + g + 1) * HEAD_DIM] for g in range(GROUP)],
            axis=0)
        kk = jnp.concatenate([kp_ref[:, lanes], kc_ref[:, lanes]], axis=0)
        vv = jnp.concatenate([vp_ref[:, lanes], vc_ref[:, lanes]], axis=0)
        s = _dot_nt(q_st, kk)
        ps, dens = [], []
        for g in range(GROUP):
            head = kv * GROUP + g
            sg = jnp.where(valid, s[g * w:(g + 1) * w], -jnp.inf)
            sink = sink_ref[0:1, head:head + 1]
            m = jnp.maximum(jnp.max(sg, axis=-1, keepdims=True), sink)
            p = jnp.exp(sg - m)
            dens.append(jnp.sum(p, axis=-1, keepdims=True) + jnp.exp(sink - m))
            ps.append(p.astype(BF16))
        out = _dot(jnp.concatenate(ps, axis=0), vv) / jnp.concatenate(dens, axis=0)
        for g in range(GROUP):
            head = kv * GROUP + g
            o_ref[:, head * HEAD_DIM:(head + 1) * HEAD_DIM] = out[g * w:(g + 1) * w].astype(BF16)


def _swa_prompt(q, kb, vb, sink, batch, t):
    nb = t // WINDOW
    cur = lambda n: pl.BlockSpec((WINDOW, n), lambda b, i: (b * nb + i, 0))
    prev = lambda n: pl.BlockSpec((WINDOW, n), lambda b, i: (b * nb + jnp.maximum(i - 1, 0), 0))
    return pl.pallas_call(
        _swa_prompt_kernel, grid=(batch, nb),
        in_specs=[cur(D_MODEL), prev(D_KV), cur(D_KV), prev(D_KV), cur(D_KV), _resident((1, N_HEADS))],
        out_specs=cur(D_MODEL),
        out_shape=jax.ShapeDtypeStruct((batch * t, D_MODEL), BF16),
        compiler_params=_params(("parallel", "arbitrary")), name="swa_prompt")(q, kb, kb, vb, vb, sink)


def _swa_sample_kernel(q_ref, ck_ref, cv_ref, kn_ref, vn_ref, sink_ref, o_ref, kbuf_ref, vbuf_ref, *, n_new):
    rows = N_HEADS * n_new
    key = lax.broadcasted_iota(jnp.int32, (rows, WINDOW), 1)
    t_c = lax.broadcasted_iota(jnp.int32, (rows, WINDOW), 0) % n_new
    u = lax.broadcasted_iota(jnp.int32, (rows, NEW_PAD), 1)
    t_n = lax.broadcasted_iota(jnp.int32, (rows, NEW_PAD), 0) % n_new
    sink = sink_ref[...]
    for b in range(SWA_SEQS_PER_STEP):
        q = q_ref[b]
        ck, cv, kn, vn = ck_ref[b], cv_ref[b], kn_ref[b], vn_ref[b]
        s_c = jnp.where(key > t_c, _dot_nt(q, ck.astype(BF16)), -jnp.inf)
        s_n = jnp.where(u <= t_n, _dot_nt(q, kn.astype(BF16)), -jnp.inf)
        m = jnp.maximum(jnp.maximum(jnp.max(s_c, axis=-1, keepdims=True),
                                    jnp.max(s_n, axis=-1, keepdims=True)), sink)
        p_c = jnp.exp(s_c - m)
        p_n = jnp.exp(s_n - m)
        den = (jnp.sum(p_c, axis=-1, keepdims=True) + jnp.sum(p_n, axis=-1, keepdims=True)
               + jnp.exp(sink - m))
        o = _dot(p_c.astype(BF16), cv.astype(BF16)) + _dot(p_n.astype(BF16), vn.astype(BF16))
        o_ref[b] = o / den
        kbuf_ref[b, 0:WINDOW - n_new, :] = ck[n_new:, :]
        kbuf_ref[b, WINDOW - n_new:WINDOW, :] = kn[0:n_new, :]
        vbuf_ref[b, 0:WINDOW - n_new, :] = cv[n_new:, :]
        vbuf_ref[b, WINDOW - n_new:WINDOW, :] = vn[0:n_new, :]


def _swa_sample(q_bd, cache_k, cache_v, k_new, v_new, sink_rows, n_new):
    nseq = q_bd.shape[0]
    rows = N_HEADS * n_new
    sb = SWA_SEQS_PER_STEP
    blk = lambda a, b: pl.BlockSpec((sb, a, b), lambda i: (i, 0, 0))
    return pl.pallas_call(
        functools.partial(_swa_sample_kernel, n_new=n_new), grid=(nseq // sb,),
        in_specs=[blk(rows, D_KV), blk(WINDOW, D_KV), blk(WINDOW, D_KV), blk(NEW_PAD, D_KV), blk(NEW_PAD, D_KV),
                  _resident((rows, 1))],
        out_specs=[blk(rows, D_KV), blk(WINDOW, D_KV), blk(WINDOW, D_KV)],
        out_shape=[jax.ShapeDtypeStruct((nseq, rows, D_KV), F32),
                   jax.ShapeDtypeStruct((nseq, WINDOW, D_KV), F32),
                   jax.ShapeDtypeStruct((nseq, WINDOW, D_KV), F32)],
        compiler_params=_params(("parallel",)), name="swa_sample")(
            q_bd, cache_k, cache_v, k_new, v_new, sink_rows)


def _ret_proj_kernel(h_ref, g_ref, w_ref, cos_ref, sin_ref, q_ref, k_ref, v_ref, gate_ref):
    xn = _rms(h_ref[...], g_ref[...]).astype(BF16)
    cos = jnp.concatenate([cos_ref[...]] * RET_HEADS, axis=1)
    sin = jnp.concatenate([sin_ref[...]] * RET_HEADS, axis=1)
    d = D_MODEL
    q_ref[...] = _dot(xn, w_ref[:, 0:d]) * cos + _dot(xn, w_ref[:, d:2 * d]) * sin
    k_scale = RET_DK ** -0.5
    k_ref[...] = (_dot(xn, w_ref[:, 2 * d:3 * d]) * k_scale) * cos + (_dot(xn, w_ref[:, 3 * d:4 * d]) * k_scale) * sin
    v_ref[...] = _dot(xn, w_ref[:, 4 * d:6 * d])
    z = _dot(xn, w_ref[:, 6 * d:8 * d])
    gate_ref[...] = z / (1.0 + jnp.exp(-z))


def _ret_proj(h, g, w, cos_tab, sin_tab, prompt_tiles, tiles_per_seq):
    m = h.shape[0]
    row = lambda n: pl.BlockSpec((ROW_TILE, n), lambda i: (i, 0))
    tab = pl.BlockSpec((ROW_TILE, RET_DK),
                       lambda i: (jnp.where(i < prompt_tiles, i % tiles_per_seq, tiles_per_seq), 0))
    dv = RET_HEADS * RET_DV
    return pl.pallas_call(
        _ret_proj_kernel, grid=(m // ROW_TILE,),
        in_specs=[row(D_MODEL), _resident((1, D_MODEL)), _resident(w.shape), tab, tab],
        out_specs=[row(D_MODEL), row(D_MODEL), row(dv), row(dv)],
        out_shape=[jax.ShapeDtypeStruct((m, D_MODEL), F32), jax.ShapeDtypeStruct((m, D_MODEL), F32),
                   jax.ShapeDtypeStruct((m, dv), F32), jax.ShapeDtypeStruct((m, dv), F32)],
        compiler_params=_params(("parallel",)), name="ret_proj")(h, g, w, cos_tab, sin_tab)


def _ret_chunk_kernel(*refs, has_init):
    if has_init:
        q_ref, k_ref, v_ref, gate_ref, gn_ref, decay_ref, xi_ref, zeta_ref, gl_ref, r0_ref, y_ref, r_ref, r_s = refs
    else:
        q_ref, k_ref, v_ref, gate_ref, gn_ref, decay_ref, xi_ref, zeta_ref, gl_ref, y_ref, r_ref, r_s = refs

    @pl.when(pl.program_id(2) == 0)
    def _():
        r_s[...] = r0_ref[...] if has_init else jnp.zeros_like(r_s)

    q = q_ref[...]
    k = k_ref[...]
    vb = v_ref[...].astype(BF16)
    r = r_s[...]
    s = _dot_nt(q.astype(BF16), k.astype(BF16)) * decay_ref[...]
    o = _dot(s.astype(BF16), vb) + _dot((q * xi_ref[...]).astype(BF16), r.astype(BF16))
    kz = (k * zeta_ref[...]).astype(BF16)
    kz_t = _dot_nt(_eye(RET_DK, BF16), kz).astype(BF16)
    r_new = r * gl_ref[...] + _dot(kz_t, vb)
    r_s[...] = r_new
    r_ref[...] = r_new

    mu = jnp.mean(o, axis=-1, keepdims=True)
    oc = o - mu
    var = jnp.mean(oc * oc, axis=-1, keepdims=True)
    y = oc * lax.rsqrt(var + EPS) * gn_ref[...]
    y_ref[...] = (gate_ref[...] * y).astype(BF16)


def _ret_chunks(q, k, v, gate, gn, tables, batch, nchunk, chunk, r0=None):
    decay, xi, zeta, gl = tables
    has_init = r0 is not None
    tok = lambda n: pl.BlockSpec((chunk, n), lambda b, h, c: (b * nchunk + c, h))
    per_head = lambda a, b_: pl.BlockSpec((None, a, b_), lambda b, h, c: (h, 0, 0))
    state = pl.BlockSpec((None, None, RET_DK, RET_DV), lambda b, h, c: (b, h, 0, 0))
    in_specs = [tok(RET_DK), tok(RET_DK), tok(RET_DV), tok(RET_DV),
                pl.BlockSpec((1, RET_DV), lambda b, h, c: (0, h)),
                per_head(chunk, chunk), per_head(chunk, 1), per_head(chunk, 1), per_head(1, 1)]
    args = [q, k, v, gate, gn, decay, xi, zeta, gl]
    if has_init:
        in_specs.append(state)
        args.append(r0)
    return pl.pallas_call(
        functools.partial(_ret_chunk_kernel, has_init=has_init), grid=(batch, RET_HEADS, nchunk),
        in_specs=in_specs, out_specs=[tok(RET_DV), state],
        out_shape=[jax.ShapeDtypeStruct((batch * nchunk * chunk, RET_HEADS * RET_DV), BF16),
                   jax.ShapeDtypeStruct((batch, RET_HEADS, RET_DK, RET_DV), F32)],
        scratch_shapes=[pltpu.VMEM((RET_DK, RET_DV), F32)],
        compiler_params=_params(("parallel", "parallel", "arbitrary")), name="ret_chunks")(*args)


def _ret_tables(chunk, n_valid):
    lg = jnp.log(1.0 - 2.0 ** (-5.0 - jnp.arange(RET_HEADS, dtype=F32)))
    idx = jnp.arange(chunk, dtype=F32)
    rel = idx[:, None] - idx[None, :]
    decay = jnp.where(rel >= 0, jnp.exp(lg[:, None, None] * jnp.maximum(rel, 0.0)), 0.0)
    xi = jnp.exp(lg[:, None] * (idx[None, :] + 1.0))[:, :, None]
    zeta = jnp.where(idx[None, :] < n_valid, jnp.exp(lg[:, None] * (n_valid - 1.0 - idx[None, :])), 0.0)[:, :, None]
    gl = jnp.exp(lg * n_valid)[:, None, None]
    return decay, xi, zeta, gl


def _rotation_tables(pos):
    freqs = 1.0 / (RET_THETA_BASE ** jnp.linspace(0.0, 1.0, RET_DK // 2, dtype=F32))
    ang = pos.astype(F32)[:, None] * freqs[None, :]
    cos = jnp.repeat(jnp.cos(ang), 2, axis=1)
    sign = jnp.tile(jnp.array([-1.0, 1.0], F32), RET_DK // 2)
    sin = jnp.repeat(jnp.sin(ang), 2, axis=1) * sign[None, :]
    return cos, sin


def _swap_pairs(w):
    d_in, d_out = w.shape
    return w.reshape(d_in, d_out // 2, 2)[:, :, ::-1].reshape(d_in, d_out)


def _post_kernel(o_ref, h_ref, p_ref, wo_ref, gm_ref, up_ref, dn_ref, gp_ref, wg_ref, wp_ref, gf_ref, out_ref,
                 *, final):
    h = h_ref[...] + _dot(o_ref[...], wo_ref[...])
    xn = _rms(h, gm_ref[...]).astype(BF16)
    mlp = jnp.zeros_like(h)
    for c in range(D_FF // D_MODEL):
        cols = slice(c * D_MODEL, (c + 1) * D_MODEL)
        u = jnp.maximum(_dot(xn, up_ref[:, cols]), 0.0)
        mlp = mlp + _dot((u * u).astype(BF16), dn_ref[cols, :])
    h = h + mlp
    xn = _rms(h, gp_ref[...]).astype(BF16)
    gate = 1.0 / (1.0 + jnp.exp(-_dot(xn, wg_ref[...])))
    h = h + gate * _dot(p_ref[...].astype(BF16), wp_ref[...])
    if final:
        h = _rms(h, gf_ref[...])
    out_ref[...] = h


def _post(o, h, p, wo, gm, up, dn, gp, wg, wp, gf, final):
    m = h.shape[0]
    row = lambda n: pl.BlockSpec((ROW_TILE, n), lambda i: (i, 0))
    vec = _resident((1, D_MODEL))
    return pl.pallas_call(
        functools.partial(_post_kernel, final=final), grid=(m // ROW_TILE,),
        in_specs=[row(o.shape[1]), row(D_MODEL), row(D_PLE), _resident(wo.shape), vec, _resident(up.shape),
                  _resident(dn.shape), vec, _resident(wg.shape), _resident(wp.shape), vec],
        out_specs=row(D_MODEL), out_shape=jax.ShapeDtypeStruct((m, D_MODEL), F32),
        compiler_params=_params(("parallel",)), name="post_mixer")(o, h, p, wo, gm, up, dn, gp, wg, wp, gf)


def _block_diag_queries(q, nseq, t):
    q5 = q.reshape(nseq, t, N_KV, GROUP, HEAD_DIM).transpose(0, 2, 3, 1, 4)
    q4 = q5.reshape(nseq, N_KV, GROUP * t, 1, HEAD_DIM)
    eye = jnp.eye(N_KV, dtype=q.dtype)[None, :, None, :, None]
    return (q4 * eye).reshape(nseq, N_KV * GROUP * t, D_KV)


def _block_diag_outputs(o, nseq, t):
    o5 = o.reshape(nseq, N_KV, GROUP * t, N_KV, HEAD_DIM)
    sel = jnp.stack([o5[:, kv, :, kv, :] for kv in range(N_KV)], axis=1)
    return sel.reshape(nseq, N_KV, GROUP, t, HEAD_DIM).transpose(0, 3, 1, 2, 4).reshape(nseq * t, D_MODEL)


def _pad_tokens(x, nseq, t, width):
    return jnp.pad(x.reshape(nseq, t, width), ((0, 0), (0, NEW_PAD - t), (0, 0)))


def kernel(x_prompt, x_sample, cache_fox_k, cache_fox_v, cache_fox_logf, cache_swa_k, cache_swa_v, state_ret, page_table, p_prompt, p_sample, fox_wq, fox_wk, fox_wv, fox_wf, fox_bf, fox_wo, swa_wq, swa_wk, swa_wv, swa_sink, swa_wo, ret_wq, ret_wk, ret_wv, ret_wg, ret_gn, ret_wo, norm_mix, norm_mlp, mlp_up, mlp_down, norm_ple, ple_gate, ple_proj, norm_final):
    batch, t_p, d = x_prompt.shape
    nseq, t_s, _ = x_sample.shape
    depth = norm_mix.shape[0]
    m_p = batch * t_p
    m_s = nseq * t_s
    n_pool = cache_fox_k.shape[1]
    past = page_table.shape[1] * PAGE
    assert d == D_MODEL and (m_p + m_s) % ROW_TILE == 0 and m_p % ROW_TILE == 0 and m_s == ROW_TILE
    assert t_p % ROW_TILE == 0 and t_s <= NEW_PAD and N_HEADS * t_s == 64

    h = jnp.concatenate([x_prompt.reshape(m_p, d), x_sample.reshape(m_s, d)], axis=0)
    p_all = jnp.concatenate([p_prompt.reshape(depth, m_p, D_PLE), p_sample.reshape(depth, m_s, D_PLE)], axis=1)
    vec = lambda a: a.reshape(1, -1)

    fox_cache_kt = cache_fox_k.transpose(0, 1, 3, 4, 2).reshape(-1, D_KV, PAGE)
    fox_cache_vt = cache_fox_v.transpose(0, 1, 3, 4, 2).reshape(-1, D_KV, PAGE)
    fox_cache_lft = cache_fox_logf.transpose(0, 1, 3, 2).reshape(-1, N_HEADS, PAGE)

    outs = {name: [] for name in ("fkp", "fvp", "flp", "fks", "fvs", "fls", "skp", "svp", "sks", "svs", "rsp", "rss")}

    for i in range(depth):
        kind, j = i % N_MIXERS, i // N_MIXERS
        g_mix = vec(norm_mix[i])
        if kind == 0:
            wq_pad, sel = _fox_query_weights(fox_wq[j])
            wf_t = jnp.pad(fox_wf[j].T, ((0, LANES - N_HEADS), (0, 0)))
            w_t = jnp.concatenate([fox_wk[j].T, fox_wv[j].T, wf_t], axis=0).astype(BF16)
            q_pad, k_t, v_t, lf_t = _fox_proj(h, g_mix, wq_pad, sel, w_t, fox_bf[j].reshape(N_HEADS, 1))
            c_parts = _cumsum_parts(lf_t, batch, t_p)
            kt_pad, vt_pad = _fox_padded_keys_values(k_t, v_t, c_parts, m_p)
            o_p = _fox_prompt(q_pad, kt_pad, vt_pad, batch, t_p)
            o_bd = _fox_sample(_fox_sample_queries(q_pad[m_p:], nseq, t_s),
                               _new_token_columns(k_t, m_p, nseq, t_s), _new_token_columns(v_t, m_p, nseq, t_s),
                               _new_token_columns(lf_t, m_p, nseq, t_s),
                               fox_cache_kt, fox_cache_vt, fox_cache_lft, page_table + j * n_pool, t_s)
            o_s = _fox_sample_outputs(o_bd, nseq, t_s).astype(BF16)
            o = jnp.concatenate([o_p, o_s], axis=0)
            wo = fox_wo[j]
            prompt_rows = lambda x_t, n: x_t[:, :m_p].reshape(n, batch, t_p).transpose(1, 2, 0)
            sample_rows = lambda x_t, n: x_t[:, m_p:].reshape(n, nseq, t_s).transpose(1, 2, 0)
            outs["fkp"].append(prompt_rows(k_t, D_KV).reshape(batch, t_p, N_KV, HEAD_DIM))
            outs["fvp"].append(prompt_rows(v_t, D_KV).reshape(batch, t_p, N_KV, HEAD_DIM))
            outs["flp"].append(prompt_rows(lf_t, N_HEADS))
            outs["fks"].append(sample_rows(k_t, D_KV).reshape(nseq, t_s, N_KV, HEAD_DIM))
            outs["fvs"].append(sample_rows(v_t, D_KV).reshape(nseq, t_s, N_KV, HEAD_DIM))
            outs["fls"].append(sample_rows(lf_t, N_HEADS))
        elif kind == 1:
            w = jnp.concatenate([swa_wq[j], swa_wk[j], swa_wv[j]], axis=1).astype(BF16)
            q, k, v, kb, vb = _swa_proj(h, g_mix, w)
            o_p = _swa_prompt(q, kb, vb, vec(swa_sink[j]), batch, t_p)
            q_bd = _block_diag_queries(q[m_p:], nseq, t_s)
            sink_rows = jnp.repeat(swa_sink[j], t_s).reshape(N_HEADS * t_s, 1)
            win = cache_swa_k.shape[2]
            o_bd, kbuf, vbuf = _swa_sample(
                q_bd, cache_swa_k[j].reshape(nseq, win, D_KV), cache_swa_v[j].reshape(nseq, win, D_KV),
                _pad_tokens(k[m_p:], nseq, t_s, D_KV), _pad_tokens(v[m_p:], nseq, t_s, D_KV), sink_rows, t_s)
            o_s = _block_diag_outputs(o_bd, nseq, t_s).astype(BF16)
            o = jnp.concatenate([o_p, o_s], axis=0)
            wo = swa_wo[j]
            wp = min(WINDOW, t_p)
            outs["skp"].append(k[:m_p].reshape(batch, t_p, N_KV, HEAD_DIM)[:, -wp:])
            outs["svp"].append(v[:m_p].reshape(batch, t_p, N_KV, HEAD_DIM)[:, -wp:])
            outs["sks"].append(kbuf.reshape(nseq, win, N_KV, HEAD_DIM))
            outs["svs"].append(vbuf.reshape(nseq, win, N_KV, HEAD_DIM))
        else:
            w = jnp.concatenate([ret_wq[j], _swap_pairs(ret_wq[j]), ret_wk[j], _swap_pairs(ret_wk[j]),
                                 ret_wv[j], ret_wg[j]], axis=1).astype(BF16)
            cos_p, sin_p = _rotation_tables(jnp.arange(t_p))
            cos_s, sin_s = _rotation_tables(past + jnp.arange(t_s))
            cos_tab = jnp.concatenate([cos_p, jnp.tile(cos_s, (nseq, 1))], axis=0)
            sin_tab = jnp.concatenate([sin_p, jnp.tile(sin_s, (nseq, 1))], axis=0)
            qr, kr, v, gate = _ret_proj(h, g_mix, w, cos_tab, sin_tab, m_p // ROW_TILE, t_p // ROW_TILE)
            gn = vec(ret_gn[j])
            y_p, r_p = _ret_chunks(qr, kr, v, gate, gn, _ret_tables(RET_CHUNK, RET_CHUNK),
                                   batch, t_p // RET_CHUNK, RET_CHUNK)
            dv = RET_HEADS * RET_DV
            pad = lambda a, n: _pad_tokens(a[m_p:], nseq, t_s, n).reshape(nseq * NEW_PAD, n)
            y_s, r_s = _ret_chunks(pad(qr, D_MODEL), pad(kr, D_MODEL), pad(v, dv), pad(gate, dv), gn,
                                   _ret_tables(NEW_PAD, t_s), nseq, 1, NEW_PAD,
                                   r0=state_ret[j].astype(F32))
            y_s = y_s.reshape(nseq, NEW_PAD, dv)[:, :t_s].reshape(m_s, dv)
            o = jnp.concatenate([y_p, y_s], axis=0)
            wo = ret_wo[j]
            outs["rsp"].append(r_p)
            outs["rss"].append(r_s)
        h = _post(o, h, p_all[i], wo.astype(BF16), vec(norm_mlp[i]), mlp_up[i].astype(BF16),
                  mlp_down[i].astype(BF16), vec(norm_ple[i]), ple_gate[i].astype(BF16),
                  ple_proj[i].astype(BF16), vec(norm_final), final=(i == depth - 1))

    y_prompt = h[:m_p].reshape(batch, t_p, d)
    y_sample = h[m_p:].reshape(nseq, t_s, d)
    st = lambda name: jnp.stack(outs[name])
    return (y_prompt, y_sample, st("fkp"), st("fvp"), st("flp"), st("fks"), st("fvs"), st("fls"),
            st("skp"), st("svp"), st("sks"), st("svs"), st("rsp"), st("rss"))
```

```python
import functools

import jax
import jax.numpy as jnp
from jax import lax
from jax.experimental import pallas as pl
from jax.experimental.pallas import tpu as pltpu

F32 = jnp.float32
BF16 = jnp.bfloat16

D_MODEL = 1024
N_HEADS = 16
N_KV = 4
GROUP = 4
HEAD_DIM = 64
D_KV = N_KV * HEAD_DIM
ATTN_SCALE = HEAD_DIM ** -0.5
WINDOW = 128
PAGE = 128
RET_HEADS = 4
RET_DK = 256
RET_DV = 512
RET_CHUNK = 128
RET_THETA_BASE = 10000.0
D_FF = 4096
D_PLE = 256
EPS = 1e-6
N_MIXERS = 3

LANES = 128
ROW_TILE = 512
FOX_Q_BLOCK = 256
FOX_K_BLOCK = 1024
FOX_KV_PER_LOOP = 2
FOX_ROW_CHUNK = 256
CUMSUM_BLOCK = 512
PAGES_PER_STEP = 16
NEW_PAD = 16
SWA_SEQS_PER_STEP = 8
VMEM_LIMIT = 56 * 1024 * 1024

FOX_BIAS_OFFSET = HEAD_DIM
FOX_ONES_ROW = HEAD_DIM

NT_DIMS = (((1,), (1,)), ((), ()))


def _params(semantics):
    return pltpu.CompilerParams(dimension_semantics=semantics, vmem_limit_bytes=VMEM_LIMIT)


def _resident(shape):
    zeros = (0,) * len(shape)
    return pl.BlockSpec(shape, lambda *_: zeros, pipeline_mode=pl.Buffered(1))


def _rms(x, g):
    ms = jnp.mean(x * x, axis=-1, keepdims=True)
    return x * lax.rsqrt(ms + EPS) * g


def _dot(a, b):
    return jnp.dot(a, b, preferred_element_type=F32)


def _dot_nt(a, b):
    return lax.dot_general(a, b, NT_DIMS, preferred_element_type=F32)


def _split3(x):
    hi = x.astype(BF16)
    r = x - hi.astype(F32)
    mid = r.astype(BF16)
    lo = (r - mid.astype(F32)).astype(BF16)
    return hi, mid, lo


def _eye(n, dtype):
    r = lax.broadcasted_iota(jnp.int32, (n, n), 0)
    c = lax.broadcasted_iota(jnp.int32, (n, n), 1)
    return (r == c).astype(dtype)


def _upper_tri(n):
    r = lax.broadcasted_iota(jnp.int32, (n, n), 0)
    c = lax.broadcasted_iota(jnp.int32, (n, n), 1)
    return (r <= c).astype(BF16)


def _log_sigmoid(z):
    return jnp.minimum(z, 0.0) - jnp.log1p(jnp.exp(-jnp.abs(z)))


def _swa_proj_kernel(h_ref, g_ref, w_ref, q_ref, k_ref, v_ref, kb_ref, vb_ref):
    xn = _rms(h_ref[...], g_ref[...]).astype(BF16)
    y = _dot(xn, w_ref[...])
    q_ref[...] = (y[:, :D_MODEL] * ATTN_SCALE).astype(BF16)
    k = y[:, D_MODEL:D_MODEL + D_KV]
    v = y[:, D_MODEL + D_KV:D_MODEL + 2 * D_KV]
    k_ref[...] = k
    v_ref[...] = v
    kb_ref[...] = k.astype(BF16)
    vb_ref[...] = v.astype(BF16)


def _swa_proj(h, g, w):
    m = h.shape[0]
    row = lambda n: pl.BlockSpec((ROW_TILE, n), lambda i: (i, 0))
    return pl.pallas_call(
        _swa_proj_kernel, grid=(m // ROW_TILE,),
        in_specs=[row(D_MODEL), _resident((1, D_MODEL)), _resident(w.shape)],
        out_specs=[row(D_MODEL), row(D_KV), row(D_KV), row(D_KV), row(D_KV)],
        out_shape=[jax.ShapeDtypeStruct((m, D_MODEL), BF16),
                   jax.ShapeDtypeStruct((m, D_KV), F32), jax.ShapeDtypeStruct((m, D_KV), F32),
                   jax.ShapeDtypeStruct((m, D_KV), BF16), jax.ShapeDtypeStruct((m, D_KV), BF16)],
        compiler_params=_params(("parallel",)), name="swa_proj")(h, g, w)


def _fox_proj_kernel(h_ref, g_ref, wq_ref, sel_ref, wt_ref, bf_ref, q_ref, kt_ref, vt_ref, lft_ref):
    xn = _rms(h_ref[...], g_ref[...]).astype(BF16)
    q_ref[...] = (_dot(xn, wq_ref[...]) * ATTN_SCALE + sel_ref[...]).astype(BF16)
    t = _dot_nt(wt_ref[...], xn)
    kt_ref[...] = t[0:D_KV]
    vt_ref[...] = t[D_KV:2 * D_KV]
    lft_ref[...] = _log_sigmoid(t[2 * D_KV:2 * D_KV + N_HEADS] + bf_ref[...])


def _fox_proj(h, g, wq_pad, sel, w_t, bf_col):
    m = h.shape[0]
    row = lambda n: pl.BlockSpec((ROW_TILE, n), lambda i: (i, 0))
    col = lambda n: pl.BlockSpec((n, ROW_TILE), lambda i: (0, i))
    return pl.pallas_call(
        _fox_proj_kernel, grid=(m // ROW_TILE,),
        in_specs=[row(D_MODEL), _resident((1, D_MODEL)), _resident(wq_pad.shape), _resident(sel.shape),
                  _resident(w_t.shape), _resident(bf_col.shape)],
        out_specs=[row(N_HEADS * LANES), col(D_KV), col(D_KV), col(N_HEADS)],
        out_shape=[jax.ShapeDtypeStruct((m, N_HEADS * LANES), BF16),
                   jax.ShapeDtypeStruct((D_KV, m), F32), jax.ShapeDtypeStruct((D_KV, m), F32),
                   jax.ShapeDtypeStruct((N_HEADS, m), F32)],
        compiler_params=_params(("parallel",)), name="fox_proj")(h, g, wq_pad, sel, w_t, bf_col)


def _fox_query_weights(wq):
    d = wq.shape[0]
    w = jnp.pad(wq.reshape(d, N_HEADS, HEAD_DIM), ((0, 0), (0, 0), (0, LANES - HEAD_DIM)))
    lane = jnp.arange(LANES)[None, :]
    g = (jnp.arange(N_HEADS) % GROUP)[:, None]
    first = FOX_BIAS_OFFSET + 3 * g
    sel = jnp.where((lane >= first) & (lane < first + 3), -1.0, 0.0).astype(F32)
    return w.reshape(d, N_HEADS * LANES).astype(BF16), sel.reshape(1, N_HEADS * LANES)


def _cumsum_kernel(lf_ref, out_ref, carry):
    @pl.when(pl.program_id(1) == 0)
    def _():
        carry[...] = jnp.zeros_like(carry)

    tri = _upper_tri(CUMSUM_BLOCK)
    c = carry[...]
    for part in _split3(lf_ref[...]):
        c = c + _dot(part, tri)
    out_ref[...] = jnp.concatenate(_split3(c), axis=0)
    carry[...] = c[:, CUMSUM_BLOCK - 1:CUMSUM_BLOCK]


def _cumsum_parts(lf_t, batch, t):
    nblk = t // CUMSUM_BLOCK
    return pl.pallas_call(
        _cumsum_kernel, grid=(batch, nblk),
        in_specs=[pl.BlockSpec((N_HEADS, CUMSUM_BLOCK), lambda b, c: (0, b * nblk + c))],
        out_specs=pl.BlockSpec((3 * N_HEADS, CUMSUM_BLOCK), lambda b, c: (0, b * nblk + c)),
        out_shape=jax.ShapeDtypeStruct((3 * N_HEADS, batch * t), BF16),
        scratch_shapes=[pltpu.VMEM((N_HEADS, 1), F32)],
        compiler_params=_params(("parallel", "arbitrary")), name="fox_cumsum")(lf_t)


def _fox_prompt_kernel(q_ref, k_ref, v_ref, o_ref, m_s, acc_s):
    tq, tk = FOX_Q_BLOCK, FOX_K_BLOCK
    rows = GROUP * tq
    i = pl.program_id(1)
    last = (i * tq) // tk
    q_local = lax.broadcasted_iota(jnp.int32, (rows, tk), 0) % tq
    key_local = lax.broadcasted_iota(jnp.int32, (rows, tk), 1)
    causal = key_local - q_local <= i * tq - last * tk

    for pair in range(N_KV // FOX_KV_PER_LOOP):
        kvs = [pair * FOX_KV_PER_LOOP + x for x in range(FOX_KV_PER_LOOP)]
        q_st = [jnp.concatenate(
            [q_ref[:, (kv * GROUP + g) * LANES:(kv * GROUP + g + 1) * LANES] for g in range(GROUP)], axis=0)
            for kv in kvs]
        m_s[...] = jnp.full_like(m_s, -jnp.inf)
        acc_s[...] = jnp.zeros_like(acc_s)

        def step(j, masked):
            start = pl.multiple_of(j * tk, tk)
            keys = pl.ds(start, tk)
            chunks = [(x, kv, slice(c * FOX_ROW_CHUNK, (c + 1) * FOX_ROW_CHUNK))
                      for x, kv in enumerate(kvs) for c in range(rows // FOX_ROW_CHUNK)]

            def scores(x, kv, rs):
                return _dot(q_st[x][rs], k_ref[kv * LANES:(kv + 1) * LANES, keys])

            s_next = scores(*chunks[0])
            for n, (x, kv, rs) in enumerate(chunks):
                s = s_next
                if n + 1 < len(chunks):
                    s_next = scores(*chunks[n + 1])
                if masked:
                    s = jnp.where(causal[rs], s, -jnp.inf)
                m_old = m_s[x, rs]
                m_new = jnp.maximum(m_old, jnp.max(s, axis=-1, keepdims=True))
                p = jnp.exp(s - m_new).astype(BF16)
                m_s[x, rs] = m_new
                acc_s[x, rs] = (jnp.exp(m_old - m_new) * acc_s[x, rs]
                                + _dot_nt(p, v_ref[kv * LANES:(kv + 1) * LANES, keys]))

        def full_step(j, carry):
            step(j, False)
            return carry

        lax.fori_loop(0, last, full_step, 0)
        step(last, True)
        for x, kv in enumerate(kvs):
            acc = acc_s[x]
            out = acc[:, :HEAD_DIM] / acc[:, FOX_ONES_ROW:FOX_ONES_ROW + 1]
            for g in range(GROUP):
                head = kv * GROUP + g
                o_ref[:, head * HEAD_DIM:(head + 1) * HEAD_DIM] = out[g * tq:(g + 1) * tq].astype(BF16)


def _fox_prompt(q_pad, kt_pad, vt_pad, batch, t):
    nq = t // FOX_Q_BLOCK
    seq = pl.BlockSpec((N_KV * LANES, t), lambda b, i: (0, b), pipeline_mode=pl.Buffered(1))
    return pl.pallas_call(
        _fox_prompt_kernel, grid=(batch, nq),
        in_specs=[pl.BlockSpec((FOX_Q_BLOCK, N_HEADS * LANES), lambda b, i: (b * nq + i, 0)), seq, seq],
        out_specs=pl.BlockSpec((FOX_Q_BLOCK, D_MODEL), lambda b, i: (b * nq + i, 0)),
        out_shape=jax.ShapeDtypeStruct((batch * t, D_MODEL), BF16),
        scratch_shapes=[pltpu.VMEM((FOX_KV_PER_LOOP, GROUP * FOX_Q_BLOCK, 1), F32),
                        pltpu.VMEM((FOX_KV_PER_LOOP, GROUP * FOX_Q_BLOCK, LANES), F32)],
        compiler_params=_params(("parallel", "arbitrary")), name="fox_prompt")(q_pad, kt_pad, vt_pad)


def _fox_padded_keys_values(k_t, v_t, c_parts, m_p):
    kt = k_t[:, :m_p].astype(BF16).reshape(N_KV, HEAD_DIM, m_p)
    vt = v_t[:, :m_p].astype(BF16).reshape(N_KV, HEAD_DIM, m_p)
    cp = c_parts.reshape(3, N_KV, GROUP, m_p).transpose(1, 2, 0, 3).reshape(N_KV, 3 * GROUP, m_p)
    k_fill = jnp.zeros((N_KV, LANES - HEAD_DIM - 3 * GROUP, m_p), BF16)
    v_fill = jnp.zeros((N_KV, LANES - HEAD_DIM - 1, m_p), BF16)
    ones = jnp.ones((N_KV, 1, m_p), BF16)
    kt_pad = jnp.concatenate([kt, cp, k_fill], axis=1).reshape(N_KV * LANES, m_p)
    vt_pad = jnp.concatenate([vt, ones, v_fill], axis=1).reshape(N_KV * LANES, m_p)
    return kt_pad, vt_pad


def _lane_cumsum(lf, tri, carry, width, n, repeat):
    heads = lf.shape[0]
    parts = _split3(lf)
    stacked = jnp.concatenate([p[:, i * width:(i + 1) * width] for i in range(n) for p in parts], axis=0)
    cs = _dot(stacked, tri)
    key_chunk = lax.broadcasted_iota(jnp.int32, (n * width, LANES), 0) // width
    col = lax.broadcasted_iota(jnp.int32, (n * width, LANES), 1)
    before = (key_chunk < col).astype(BF16)
    offsets = carry + (_dot(parts[0], before) + _dot(parts[1], before) + _dot(parts[2], before))
    out = []
    for i in range(n):
        base = 3 * heads * i
        g = cs[base:base + heads] + cs[base + heads:base + 2 * heads] + cs[base + 2 * heads:base + 3 * heads]
        g = g + offsets[:, i:i + 1]
        out.append(jnp.concatenate([g] * repeat, axis=0))
    return out, offsets[:, n:n + 1]


def _softmax_step(s, v_t, m_s, l_s, acc_s):
    m_old = m_s[...]
    m_new = jnp.maximum(m_old, jnp.max(s, axis=-1, keepdims=True))
    p = jnp.exp(s - m_new)
    alpha = jnp.exp(m_old - m_new)
    l_s[...] = alpha * l_s[...] + jnp.sum(p, axis=-1, keepdims=True)
    m_s[...] = m_new
    acc_s[...] = alpha * acc_s[...] + _dot_nt(p.astype(BF16), v_t)


def _fox_sample_kernel(pt_ref, q_ref, kn_ref, vn_ref, lfn_ref, *rest, n_new):
    del pt_ref
    pp = PAGES_PER_STEP
    k_refs, v_refs, lf_refs = rest[:pp], rest[pp:2 * pp], rest[2 * pp:3 * pp]
    o_ref, m_s, l_s, acc_s, g_s, kcat, vcat = rest[3 * pp:]
    step = pl.program_id(1)
    rows = N_HEADS * n_new

    @pl.when(step == 0)
    def _():
        m_s[...] = jnp.full_like(m_s, -jnp.inf)
        l_s[...] = jnp.zeros_like(l_s)
        acc_s[...] = jnp.zeros_like(acc_s)
        g_s[...] = jnp.zeros_like(g_s)

    q = q_ref[...]
    for i in range(pp):
        kcat[:, i * PAGE:(i + 1) * PAGE] = k_refs[i][...].astype(BF16)
        vcat[:, i * PAGE:(i + 1) * PAGE] = v_refs[i][...].astype(BF16)
    lf = jnp.concatenate([lf_refs[i][...] for i in range(pp)], axis=1)
    gs, carry = _lane_cumsum(lf, _upper_tri(PAGE), g_s[...], PAGE, pp, n_new)
    g_s[...] = carry
    s = _dot(q, kcat[...]) - jnp.concatenate(gs, axis=1)
    _softmax_step(s, vcat[...], m_s, l_s, acc_s)

    @pl.when(step == pl.num_programs(1) - 1)
    def _():
        gn, _ = _lane_cumsum(lfn_ref[...], _upper_tri(NEW_PAD), carry, NEW_PAD, 1, n_new)
        s_new = _dot(q, kn_ref[...].astype(BF16)) - gn[0]
        u = lax.broadcasted_iota(jnp.int32, (rows, NEW_PAD), 1)
        t = lax.broadcasted_iota(jnp.int32, (rows, NEW_PAD), 0) // N_HEADS
        s_new = jnp.where(u <= t, s_new, -jnp.inf)
        _softmax_step(s_new, vn_ref[...].astype(BF16), m_s, l_s, acc_s)
        o_ref[...] = acc_s[...] / l_s[...]


def _fox_sample(q_bd, kn_t, vn_t, lfn_t, cache_kt, cache_vt, cache_lft, page_ids, n_new):
    nseq, npages = page_ids.shape
    pp = PAGES_PER_STEP
    rows = N_HEADS * n_new
    per_seq = lambda *shape: pl.BlockSpec((None,) + shape, lambda b, s, pt: (b, 0, 0))

    def page(height, i):
        return pl.BlockSpec((None, height, PAGE), lambda b, s, pt: (pt[b, s * pp + i], 0, 0))

    in_specs = ([per_seq(rows, D_KV), per_seq(D_KV, NEW_PAD), per_seq(D_KV, NEW_PAD), per_seq(N_HEADS, NEW_PAD)]
                + [page(D_KV, i) for i in range(pp)] + [page(D_KV, i) for i in range(pp)]
                + [page(N_HEADS, i) for i in range(pp)])
    grid_spec = pltpu.PrefetchScalarGridSpec(
        num_scalar_prefetch=1, grid=(nseq, npages // pp), in_specs=in_specs,
        out_specs=per_seq(rows, D_KV),
        scratch_shapes=[pltpu.VMEM((rows, 1), F32), pltpu.VMEM((rows, 1), F32), pltpu.VMEM((rows, D_KV), F32),
                        pltpu.VMEM((N_HEADS, 1), F32), pltpu.VMEM((D_KV, pp * PAGE), BF16),
                        pltpu.VMEM((D_KV, pp * PAGE), BF16)])
    return pl.pallas_call(
        functools.partial(_fox_sample_kernel, n_new=n_new), grid_spec=grid_spec,
        out_shape=jax.ShapeDtypeStruct((nseq, rows, D_KV), F32),
        compiler_params=_params(("parallel", "arbitrary")), name="fox_sample")(
            page_ids, q_bd, kn_t, vn_t, lfn_t, *([cache_kt] * pp), *([cache_vt] * pp), *([cache_lft] * pp))


def _fox_sample_queries(q_pad, nseq, t):
    q = q_pad.reshape(nseq, t, N_HEADS, LANES)[..., :HEAD_DIM]
    own = (jnp.arange(N_HEADS)[:, None] // GROUP == jnp.arange(N_KV)[None, :]).astype(q.dtype)
    return (q[:, :, :, None, :] * own[None, None, :, :, None]).reshape(nseq, t * N_HEADS, D_KV)


def _fox_sample_outputs(o, nseq, t):
    o6 = o.reshape(nseq, t, N_KV, GROUP, N_KV, HEAD_DIM)
    sel = jnp.stack([o6[:, :, kv, :, kv, :] for kv in range(N_KV)], axis=2)
    return sel.reshape(nseq * t, D_MODEL)


def _new_token_columns(x_t, m_p, nseq, t):
    n = x_t.shape[0]
    x = x_t[:, m_p:].reshape(n, nseq, t).transpose(1, 0, 2)
    return jnp.pad(x, ((0, 0), (0, 0), (0, NEW_PAD - t)))


def _swa_prompt_kernel(q_ref, kp_ref, kc_ref, vp_ref, vc_ref, sink_ref, o_ref):
    w = WINDOW
    i = pl.program_id(1)
    row = lax.broadcasted_iota(jnp.int32, (w, 2 * w), 0)
    col = lax.broadcasted_iota(jnp.int32, (w, 2 * w), 1)
    valid = (col > row) & (col <= row + w) & ((col >= w) | (i > 0))
    ones = jnp.ones((2 * w, LANES), BF16)

    def scores(kv):
        lanes = slice(kv * HEAD_DIM, (kv + 1) * HEAD_DIM)
        q_st = jnp.concatenate(
            [q_ref[:, (kv * GROUP + g) * HEAD_DIM:(kv * GROUP + g + 1) * HEAD_DIM] for g in range(GROUP)],
            axis=0)
        kk = jnp.concatenate([kp_ref[:, lanes], kc_ref[:, lanes]], axis=0)
        return _dot_nt(q_st, kk)

    s_next = scores(0)
    for kv in range(N_KV):
        lanes = slice(kv * HEAD_DIM, (kv + 1) * HEAD_DIM)
        s = s_next
        if kv + 1 < N_KV:
            s_next = scores(kv + 1)
        vv = jnp.concatenate([vp_ref[:, lanes], vc_ref[:, lanes]], axis=0)
        ps, sink_terms = [], []
        for g in range(GROUP):
            head = kv * GROUP + g
            sg = jnp.where(valid, s[g * w:(g + 1) * w], -jnp.inf)
            sink = sink_ref[0:1, head:head + 1]
            m = jnp.maximum(jnp.max(sg, axis=-1, keepdims=True), sink)
            ps.append(jnp.exp(sg - m).astype(BF16))
            sink_terms.append(jnp.exp(sink - m))
        p = jnp.concatenate(ps, axis=0)
        den = _dot(p, ones)[:, :HEAD_DIM] + jnp.concatenate(sink_terms, axis=0)
        out = _dot(p, vv) / den
        for g in range(GROUP):
            head = kv * GROUP + g
            o_ref[:, head * HEAD_DIM:(head + 1) * HEAD_DIM] = out[g * w:(g + 1) * w].astype(BF16)


def _swa_prompt(q, kb, vb, sink, batch, t):
    nb = t // WINDOW
    cur = lambda n: pl.BlockSpec((WINDOW, n), lambda b, i: (b * nb + i, 0))
    prev = lambda n: pl.BlockSpec((WINDOW, n), lambda b, i: (b * nb + jnp.maximum(i - 1, 0), 0))
    return pl.pallas_call(
        _swa_prompt_kernel, grid=(batch, nb),
        in_specs=[cur(D_MODEL), prev(D_KV), cur(D_KV), prev(D_KV), cur(D_KV), _resident((1, N_HEADS))],
        out_specs=cur(D_MODEL),
        out_shape=jax.ShapeDtypeStruct((batch * t, D_MODEL), BF16),
        compiler_params=_params(("parallel", "arbitrary")), name="swa_prompt")(q, kb, kb, vb, vb, sink)


def _swa_sample_kernel(q_ref, ck_ref, cv_ref, kn_ref, vn_ref, sink_ref, o_ref, kbuf_ref, vbuf_ref, *, n_new):
    rows = N_HEADS * n_new
    key = lax.broadcasted_iota(jnp.int32, (rows, WINDOW), 1)
    t_c = lax.broadcasted_iota(jnp.int32, (rows, WINDOW), 0) % n_new
    u = lax.broadcasted_iota(jnp.int32, (rows, NEW_PAD), 1)
    t_n = lax.broadcasted_iota(jnp.int32, (rows, NEW_PAD), 0) % n_new
    sink = sink_ref[...]
    for b in range(SWA_SEQS_PER_STEP):
        q = q_ref[b]
        ck, cv, kn, vn = ck_ref[b], cv_ref[b], kn_ref[b], vn_ref[b]
        s_c = jnp.where(key > t_c, _dot_nt(q, ck.astype(BF16)), -jnp.inf)
        s_n = jnp.where(u <= t_n, _dot_nt(q, kn.astype(BF16)), -jnp.inf)
        m = jnp.maximum(jnp.maximum(jnp.max(s_c, axis=-1, keepdims=True),
                                    jnp.max(s_n, axis=-1, keepdims=True)), sink)
        p_c = jnp.exp(s_c - m)
        p_n = jnp.exp(s_n - m)
        den = (jnp.sum(p_c, axis=-1, keepdims=True) + jnp.sum(p_n, axis=-1, keepdims=True)
               + jnp.exp(sink - m))
        o = _dot(p_c.astype(BF16), cv.astype(BF16)) + _dot(p_n.astype(BF16), vn.astype(BF16))
        o_ref[b] = o / den
        kbuf_ref[b, 0:WINDOW - n_new, :] = ck[n_new:, :]
        kbuf_ref[b, WINDOW - n_new:WINDOW, :] = kn[0:n_new, :]
        vbuf_ref[b, 0:WINDOW - n_new, :] = cv[n_new:, :]
        vbuf_ref[b, WINDOW - n_new:WINDOW, :] = vn[0:n_new, :]


def _swa_sample(q_bd, cache_k, cache_v, k_new, v_new, sink_rows, n_new):
    nseq = q_bd.shape[0]
    rows = N_HEADS * n_new
    sb = SWA_SEQS_PER_STEP
    blk = lambda a, b: pl.BlockSpec((sb, a, b), lambda i: (i, 0, 0))
    return pl.pallas_call(
        functools.partial(_swa_sample_kernel, n_new=n_new), grid=(nseq // sb,),
        in_specs=[blk(rows, D_KV), blk(WINDOW, D_KV), blk(WINDOW, D_KV), blk(NEW_PAD, D_KV), blk(NEW_PAD, D_KV),
                  _resident((rows, 1))],
        out_specs=[blk(rows, D_KV), blk(WINDOW, D_KV), blk(WINDOW, D_KV)],
        out_shape=[jax.ShapeDtypeStruct((nseq, rows, D_KV), F32),
                   jax.ShapeDtypeStruct((nseq, WINDOW, D_KV), F32),
                   jax.ShapeDtypeStruct((nseq, WINDOW, D_KV), F32)],
        compiler_params=_params(("parallel",)), name="swa_sample")(
            q_bd, cache_k, cache_v, k_new, v_new, sink_rows)


def _ret_proj_kernel(h_ref, g_ref, w_ref, cos_ref, sin_ref, q_ref, k_ref, v_ref, gate_ref):
    xn = _rms(h_ref[...], g_ref[...]).astype(BF16)
    cos = jnp.concatenate([cos_ref[...]] * RET_HEADS, axis=1)
    sin = jnp.concatenate([sin_ref[...]] * RET_HEADS, axis=1)
    d = D_MODEL
    q_ref[...] = _dot(xn, w_ref[:, 0:d]) * cos + _dot(xn, w_ref[:, d:2 * d]) * sin
    k_scale = RET_DK ** -0.5
    k_ref[...] = (_dot(xn, w_ref[:, 2 * d:3 * d]) * k_scale) * cos + (_dot(xn, w_ref[:, 3 * d:4 * d]) * k_scale) * sin
    v_ref[...] = _dot(xn, w_ref[:, 4 * d:6 * d])
    z = _dot(xn, w_ref[:, 6 * d:8 * d])
    gate_ref[...] = z / (1.0 + jnp.exp(-z))


def _ret_proj(h, g, w, cos_tab, sin_tab, prompt_tiles, tiles_per_seq):
    m = h.shape[0]
    row = lambda n: pl.BlockSpec((ROW_TILE, n), lambda i: (i, 0))
    tab = pl.BlockSpec((ROW_TILE, RET_DK),
                       lambda i: (jnp.where(i < prompt_tiles, i % tiles_per_seq, tiles_per_seq), 0))
    dv = RET_HEADS * RET_DV
    return pl.pallas_call(
        _ret_proj_kernel, grid=(m // ROW_TILE,),
        in_specs=[row(D_MODEL), _resident((1, D_MODEL)), _resident(w.shape), tab, tab],
        out_specs=[row(D_MODEL), row(D_MODEL), row(dv), row(dv)],
        out_shape=[jax.ShapeDtypeStruct((m, D_MODEL), F32), jax.ShapeDtypeStruct((m, D_MODEL), F32),
                   jax.ShapeDtypeStruct((m, dv), F32), jax.ShapeDtypeStruct((m, dv), F32)],
        compiler_params=_params(("parallel",)), name="ret_proj")(h, g, w, cos_tab, sin_tab)


def _ret_chunk_kernel(*refs, has_init):
    if has_init:
        q_ref, k_ref, v_ref, gate_ref, gn_ref, decay_ref, xi_ref, zeta_ref, gl_ref, r0_ref, y_ref, r_ref, r_s = refs
    else:
        q_ref, k_ref, v_ref, gate_ref, gn_ref, decay_ref, xi_ref, zeta_ref, gl_ref, y_ref, r_ref, r_s = refs

    @pl.when(pl.program_id(1) == 0)
    def _():
        r_s[...] = r0_ref[...] if has_init else jnp.zeros_like(r_s)

    eye = _eye(RET_DK, BF16)
    for hd in range(RET_HEADS):
        qk_cols = slice(hd * RET_DK, (hd + 1) * RET_DK)
        v_cols = slice(hd * RET_DV, (hd + 1) * RET_DV)
        q = q_ref[:, qk_cols]
        k = k_ref[:, qk_cols]
        vb = v_ref[:, v_cols].astype(BF16)
        r = r_s[hd]
        s = _dot_nt(q.astype(BF16), k.astype(BF16)) * decay_ref[hd]
        o = _dot(s.astype(BF16), vb) + _dot((q * xi_ref[hd]).astype(BF16), r.astype(BF16))
        kz = (k * zeta_ref[hd]).astype(BF16)
        kz_t = _dot_nt(eye, kz).astype(BF16)
        r_new = r * gl_ref[hd] + _dot(kz_t, vb)
        r_s[hd] = r_new
        r_ref[hd] = r_new

        mu = jnp.mean(o, axis=-1, keepdims=True)
        oc = o - mu
        var = jnp.mean(oc * oc, axis=-1, keepdims=True)
        y = oc * lax.rsqrt(var + EPS) * gn_ref[:, v_cols]
        y_ref[:, v_cols] = (gate_ref[:, v_cols] * y).astype(BF16)


def _ret_chunks(q, k, v, gate, gn, tables, batch, nchunk, chunk, r0=None):
    decay, xi, zeta, gl = tables
    has_init = r0 is not None
    dk, dv = RET_HEADS * RET_DK, RET_HEADS * RET_DV
    tok = lambda n: pl.BlockSpec((chunk, n), lambda b, c: (b * nchunk + c, 0))
    state = pl.BlockSpec((None, RET_HEADS, RET_DK, RET_DV), lambda b, c: (b, 0, 0, 0))
    in_specs = [tok(dk), tok(dk), tok(dv), tok(dv), _resident(gn.shape),
                _resident(decay.shape), _resident(xi.shape), _resident(zeta.shape), _resident(gl.shape)]
    args = [q, k, v, gate, gn, decay, xi, zeta, gl]
    if has_init:
        in_specs.append(state)
        args.append(r0)
    return pl.pallas_call(
        functools.partial(_ret_chunk_kernel, has_init=has_init), grid=(batch, nchunk),
        in_specs=in_specs, out_specs=[tok(dv), state],
        out_shape=[jax.ShapeDtypeStruct((batch * nchunk * chunk, dv), BF16),
                   jax.ShapeDtypeStruct((batch, RET_HEADS, RET_DK, RET_DV), F32)],
        scratch_shapes=[pltpu.VMEM((RET_HEADS, RET_DK, RET_DV), F32)],
        compiler_params=_params(("parallel", "arbitrary")), name="ret_chunks")(*args)


def _ret_tables(chunk, n_valid):
    lg = jnp.log(1.0 - 2.0 ** (-5.0 - jnp.arange(RET_HEADS, dtype=F32)))
    idx = jnp.arange(chunk, dtype=F32)
    rel = idx[:, None] - idx[None, :]
    decay = jnp.where(rel >= 0, jnp.exp(lg[:, None, None] * jnp.maximum(rel, 0.0)), 0.0)
    xi = jnp.exp(lg[:, None] * (idx[None, :] + 1.0))[:, :, None]
    zeta = jnp.where(idx[None, :] < n_valid, jnp.exp(lg[:, None] * (n_valid - 1.0 - idx[None, :])), 0.0)[:, :, None]
    gl = jnp.exp(lg * n_valid)[:, None, None]
    return decay, xi, zeta, gl


def _rotation_tables(pos):
    freqs = 1.0 / (RET_THETA_BASE ** jnp.linspace(0.0, 1.0, RET_DK // 2, dtype=F32))
    ang = pos.astype(F32)[:, None] * freqs[None, :]
    cos = jnp.repeat(jnp.cos(ang), 2, axis=1)
    sign = jnp.tile(jnp.array([-1.0, 1.0], F32), RET_DK // 2)
    sin = jnp.repeat(jnp.sin(ang), 2, axis=1) * sign[None, :]
    return cos, sin


def _swap_pairs(w):
    d_in, d_out = w.shape
    return w.reshape(d_in, d_out // 2, 2)[:, :, ::-1].reshape(d_in, d_out)


def _post_kernel(o_ref, h_ref, p_ref, wo_ref, gm_ref, up_ref, dn_ref, gp_ref, wg_ref, wp_ref, gf_ref, out_ref,
                 *, final):
    h = h_ref[...] + _dot(o_ref[...], wo_ref[...])
    xn = _rms(h, gm_ref[...]).astype(BF16)
    mlp = jnp.zeros_like(h)
    for c in range(D_FF // D_MODEL):
        cols = slice(c * D_MODEL, (c + 1) * D_MODEL)
        u = jnp.maximum(_dot(xn, up_ref[:, cols]), 0.0)
        mlp = mlp + _dot((u * u).astype(BF16), dn_ref[cols, :])
    h = h + mlp
    xn = _rms(h, gp_ref[...]).astype(BF16)
    gate = 1.0 / (1.0 + jnp.exp(-_dot(xn, wg_ref[...])))
    h = h + gate * _dot(p_ref[...].astype(BF16), wp_ref[...])
    if final:
        h = _rms(h, gf_ref[...])
    out_ref[...] = h


def _post(o, h, p, wo, gm, up, dn, gp, wg, wp, gf, final):
    m = h.shape[0]
    row = lambda n: pl.BlockSpec((ROW_TILE, n), lambda i: (i, 0))
    vec = _resident((1, D_MODEL))
    return pl.pallas_call(
        functools.partial(_post_kernel, final=final), grid=(m // ROW_TILE,),
        in_specs=[row(o.shape[1]), row(D_MODEL), row(D_PLE), _resident(wo.shape), vec, _resident(up.shape),
                  _resident(dn.shape), vec, _resident(wg.shape), _resident(wp.shape), vec],
        out_specs=row(D_MODEL), out_shape=jax.ShapeDtypeStruct((m, D_MODEL), F32),
        compiler_params=_params(("parallel",)), name="post_mixer")(o, h, p, wo, gm, up, dn, gp, wg, wp, gf)


def _block_diag_queries(q, nseq, t):
    q5 = q.reshape(nseq, t, N_KV, GROUP, HEAD_DIM).transpose(0, 2, 3, 1, 4)
    q4 = q5.reshape(nseq, N_KV, GROUP * t, 1, HEAD_DIM)
    eye = jnp.eye(N_KV, dtype=q.dtype)[None, :, None, :, None]
    return (q4 * eye).reshape(nseq, N_KV * GROUP * t, D_KV)


def _block_diag_outputs(o, nseq, t):
    o5 = o.reshape(nseq, N_KV, GROUP * t, N_KV, HEAD_DIM)
    sel = jnp.stack([o5[:, kv, :, kv, :] for kv in range(N_KV)], axis=1)
    return sel.reshape(nseq, N_KV, GROUP, t, HEAD_DIM).transpose(0, 3, 1, 2, 4).reshape(nseq * t, D_MODEL)


def _pad_tokens(x, nseq, t, width):
    return jnp.pad(x.reshape(nseq, t, width), ((0, 0), (0, NEW_PAD - t), (0, 0)))


def kernel(x_prompt, x_sample, cache_fox_k, cache_fox_v, cache_fox_logf, cache_swa_k, cache_swa_v, state_ret, page_table, p_prompt, p_sample, fox_wq, fox_wk, fox_wv, fox_wf, fox_bf, fox_wo, swa_wq, swa_wk, swa_wv, swa_sink, swa_wo, ret_wq, ret_wk, ret_wv, ret_wg, ret_gn, ret_wo, norm_mix, norm_mlp, mlp_up, mlp_down, norm_ple, ple_gate, ple_proj, norm_final):
    batch, t_p, d = x_prompt.shape
    nseq, t_s, _ = x_sample.shape
    depth = norm_mix.shape[0]
    m_p = batch * t_p
    m_s = nseq * t_s
    n_pool = cache_fox_k.shape[1]
    past = page_table.shape[1] * PAGE
    assert d == D_MODEL and (m_p + m_s) % ROW_TILE == 0 and m_p % ROW_TILE == 0 and m_s == ROW_TILE
    assert t_p % ROW_TILE == 0 and t_s <= NEW_PAD and N_HEADS * t_s == 64

    h = jnp.concatenate([x_prompt.reshape(m_p, d), x_sample.reshape(m_s, d)], axis=0)
    p_all = jnp.concatenate([p_prompt.reshape(depth, m_p, D_PLE), p_sample.reshape(depth, m_s, D_PLE)], axis=1)
    vec = lambda a: a.reshape(1, -1)

    fox_cache_kt = cache_fox_k.transpose(0, 1, 3, 4, 2).reshape(-1, D_KV, PAGE)
    fox_cache_vt = cache_fox_v.transpose(0, 1, 3, 4, 2).reshape(-1, D_KV, PAGE)
    fox_cache_lft = cache_fox_logf.transpose(0, 1, 3, 2).reshape(-1, N_HEADS, PAGE)

    outs = {name: [] for name in ("fkp", "fvp", "flp", "fks", "fvs", "fls", "skp", "svp", "sks", "svs", "rsp", "rss")}

    for i in range(depth):
        kind, j = i % N_MIXERS, i // N_MIXERS
        g_mix = vec(norm_mix[i])
        if kind == 0:
            wq_pad, sel = _fox_query_weights(fox_wq[j])
            wf_t = jnp.pad(fox_wf[j].T, ((0, LANES - N_HEADS), (0, 0)))
            w_t = jnp.concatenate([fox_wk[j].T, fox_wv[j].T, wf_t], axis=0).astype(BF16)
            q_pad, k_t, v_t, lf_t = _fox_proj(h, g_mix, wq_pad, sel, w_t, fox_bf[j].reshape(N_HEADS, 1))
            c_parts = _cumsum_parts(lf_t, batch, t_p)
            kt_pad, vt_pad = _fox_padded_keys_values(k_t, v_t, c_parts, m_p)
            o_p = _fox_prompt(q_pad, kt_pad, vt_pad, batch, t_p)
            o_bd = _fox_sample(_fox_sample_queries(q_pad[m_p:], nseq, t_s),
                               _new_token_columns(k_t, m_p, nseq, t_s), _new_token_columns(v_t, m_p, nseq, t_s),
                               _new_token_columns(lf_t, m_p, nseq, t_s),
                               fox_cache_kt, fox_cache_vt, fox_cache_lft, page_table + j * n_pool, t_s)
            o_s = _fox_sample_outputs(o_bd, nseq, t_s).astype(BF16)
            o = jnp.concatenate([o_p, o_s], axis=0)
            wo = fox_wo[j]
            prompt_rows = lambda x_t, n: x_t[:, :m_p].reshape(n, batch, t_p).transpose(1, 2, 0)
            sample_rows = lambda x_t, n: x_t[:, m_p:].reshape(n, nseq, t_s).transpose(1, 2, 0)
            outs["fkp"].append(prompt_rows(k_t, D_KV).reshape(batch, t_p, N_KV, HEAD_DIM))
            outs["fvp"].append(prompt_rows(v_t, D_KV).reshape(batch, t_p, N_KV, HEAD_DIM))
            outs["flp"].append(prompt_rows(lf_t, N_HEADS))
            outs["fks"].append(sample_rows(k_t, D_KV).reshape(nseq, t_s, N_KV, HEAD_DIM))
            outs["fvs"].append(sample_rows(v_t, D_KV).reshape(nseq, t_s, N_KV, HEAD_DIM))
            outs["fls"].append(sample_rows(lf_t, N_HEADS))
        elif kind == 1:
            w = jnp.concatenate([swa_wq[j], swa_wk[j], swa_wv[j]], axis=1).astype(BF16)
            q, k, v, kb, vb = _swa_proj(h, g_mix, w)
            o_p = _swa_prompt(q, kb, vb, vec(swa_sink[j]), batch, t_p)
            q_bd = _block_diag_queries(q[m_p:], nseq, t_s)
            sink_rows = jnp.repeat(swa_sink[j], t_s).reshape(N_HEADS * t_s, 1)
            win = cache_swa_k.shape[2]
            o_bd, kbuf, vbuf = _swa_sample(
                q_bd, cache_swa_k[j].reshape(nseq, win, D_KV), cache_swa_v[j].reshape(nseq, win, D_KV),
                _pad_tokens(k[m_p:], nseq, t_s, D_KV), _pad_tokens(v[m_p:], nseq, t_s, D_KV), sink_rows, t_s)
            o_s = _block_diag_outputs(o_bd, nseq, t_s).astype(BF16)
            o = jnp.concatenate([o_p, o_s], axis=0)
            wo = swa_wo[j]
            wp = min(WINDOW, t_p)
            outs["skp"].append(k[:m_p].reshape(batch, t_p, N_KV, HEAD_DIM)[:, -wp:])
            outs["svp"].append(v[:m_p].reshape(batch, t_p, N_KV, HEAD_DIM)[:, -wp:])
            outs["sks"].append(kbuf.reshape(nseq, win, N_KV, HEAD_DIM))
            outs["svs"].append(vbuf.reshape(nseq, win, N_KV, HEAD_DIM))
        else:
            w = jnp.concatenate([ret_wq[j], _swap_pairs(ret_wq[j]), ret_wk[j], _swap_pairs(ret_wk[j]),
                                 ret_wv[j], ret_wg[j]], axis=1).astype(BF16)
            cos_p, sin_p = _rotation_tables(jnp.arange(t_p))
            cos_s, sin_s = _rotation_tables(past + jnp.arange(t_s))
            cos_tab = jnp.concatenate([cos_p, jnp.tile(cos_s, (nseq, 1))], axis=0)
            sin_tab = jnp.concatenate([sin_p, jnp.tile(sin_s, (nseq, 1))], axis=0)
            qr, kr, v, gate = _ret_proj(h, g_mix, w, cos_tab, sin_tab, m_p // ROW_TILE, t_p // ROW_TILE)
            gn = vec(ret_gn[j])
            y_p, r_p = _ret_chunks(qr, kr, v, gate, gn, _ret_tables(RET_CHUNK, RET_CHUNK),
                                   batch, t_p // RET_CHUNK, RET_CHUNK)
            dv = RET_HEADS * RET_DV
            pad = lambda a, n: _pad_tokens(a[m_p:], nseq, t_s, n).reshape(nseq * NEW_PAD, n)
            y_s, r_s = _ret_chunks(pad(qr, D_MODEL), pad(kr, D_MODEL), pad(v, dv), pad(gate, dv), gn,
                                   _ret_tables(NEW_PAD, t_s), nseq, 1, NEW_PAD,
                                   r0=state_ret[j].astype(F32))
            y_s = y_s.reshape(nseq, NEW_PAD, dv)[:, :t_s].reshape(m_s, dv)
            o = jnp.concatenate([y_p, y_s], axis=0)
            wo = ret_wo[j]
            outs["rsp"].append(r_p)
            outs["rss"].append(r_s)
        h = _post(o, h, p_all[i], wo.astype(BF16), vec(norm_mlp[i]), mlp_up[i].astype(BF16),
                  mlp_down[i].astype(BF16), vec(norm_ple[i]), ple_gate[i].astype(BF16),
                  ple_proj[i].astype(BF16), vec(norm_final), final=(i == depth - 1))

    y_prompt = h[:m_p].reshape(batch, t_p, d)
    y_sample = h[m_p:].reshape(nseq, t_s, d)
    st = lambda name: jnp.stack(outs[name])
    return (y_prompt, y_sample, st("fkp"), st("fvp"), st("flp"), st("fks"), st("fvs"), st("fls"),
            st("skp"), st("svp"), st("sks"), st("svs"), st("rsp"), st("rss"))
```

```python
import functools

import jax
import jax.numpy as jnp
from jax import lax
from jax.experimental import pallas as pl
from jax.experimental.pallas import tpu as pltpu

F32 = jnp.float32
BF16 = jnp.bfloat16

D_MODEL = 1024
N_HEADS = 16
N_KV = 4
GROUP = 4
HEAD_DIM = 64
D_KV = N_KV * HEAD_DIM
ATTN_SCALE = HEAD_DIM ** -0.5
WINDOW = 128
PAGE = 128
RET_HEADS = 4
RET_DK = 256
RET_DV = 512
RET_CHUNK = 128
RET_THETA_BASE = 10000.0
D_FF = 4096
D_PLE = 256
EPS = 1e-6
N_MIXERS = 3

LANES = 128
ROW_TILE = 512
FOX_Q_BLOCK = 256
FOX_K_BLOCK = 1024
FOX_KV_PER_LOOP = 4
FOX_ROW_CHUNK = 256
CUMSUM_BLOCK = 512
PAGES_PER_STEP = 16
NEW_PAD = 16
SWA_SEQS_PER_STEP = 8
VMEM_LIMIT = 56 * 1024 * 1024

FOX_BIAS_OFFSET = HEAD_DIM
FOX_ONES_ROW = HEAD_DIM

NT_DIMS = (((1,), (1,)), ((), ()))


def _params(semantics):
    return pltpu.CompilerParams(dimension_semantics=semantics, vmem_limit_bytes=VMEM_LIMIT)


def _resident(shape):
    zeros = (0,) * len(shape)
    return pl.BlockSpec(shape, lambda *_: zeros, pipeline_mode=pl.Buffered(1))


def _rms(x, g):
    ms = jnp.mean(x * x, axis=-1, keepdims=True)
    return x * lax.rsqrt(ms + EPS) * g


def _dot(a, b):
    return jnp.dot(a, b, preferred_element_type=F32)


def _dot_nt(a, b):
    return lax.dot_general(a, b, NT_DIMS, preferred_element_type=F32)


def _split3(x):
    hi = x.astype(BF16)
    r = x - hi.astype(F32)
    mid = r.astype(BF16)
    lo = (r - mid.astype(F32)).astype(BF16)
    return hi, mid, lo


def _eye(n, dtype):
    r = lax.broadcasted_iota(jnp.int32, (n, n), 0)
    c = lax.broadcasted_iota(jnp.int32, (n, n), 1)
    return (r == c).astype(dtype)


def _upper_tri(n):
    r = lax.broadcasted_iota(jnp.int32, (n, n), 0)
    c = lax.broadcasted_iota(jnp.int32, (n, n), 1)
    return (r <= c).astype(BF16)


def _log_sigmoid(z):
    return jnp.minimum(z, 0.0) - jnp.log1p(jnp.exp(-jnp.abs(z)))


def _swa_proj_kernel(h_ref, g_ref, w_ref, q_ref, k_ref, v_ref, kb_ref, vb_ref):
    xn = _rms(h_ref[...], g_ref[...]).astype(BF16)
    y = _dot(xn, w_ref[...])
    q_ref[...] = (y[:, :D_MODEL] * ATTN_SCALE).astype(BF16)
    k = y[:, D_MODEL:D_MODEL + D_KV]
    v = y[:, D_MODEL + D_KV:D_MODEL + 2 * D_KV]
    k_ref[...] = k
    v_ref[...] = v
    kb_ref[...] = k.astype(BF16)
    vb_ref[...] = v.astype(BF16)


def _swa_proj(h, g, w):
    m = h.shape[0]
    row = lambda n: pl.BlockSpec((ROW_TILE, n), lambda i: (i, 0))
    return pl.pallas_call(
        _swa_proj_kernel, grid=(m // ROW_TILE,),
        in_specs=[row(D_MODEL), _resident((1, D_MODEL)), _resident(w.shape)],
        out_specs=[row(D_MODEL), row(D_KV), row(D_KV), row(D_KV), row(D_KV)],
        out_shape=[jax.ShapeDtypeStruct((m, D_MODEL), BF16),
                   jax.ShapeDtypeStruct((m, D_KV), F32), jax.ShapeDtypeStruct((m, D_KV), F32),
                   jax.ShapeDtypeStruct((m, D_KV), BF16), jax.ShapeDtypeStruct((m, D_KV), BF16)],
        compiler_params=_params(("parallel",)), name="swa_proj")(h, g, w)


def _fox_proj_kernel(h_ref, g_ref, wq_ref, sel_ref, wt_ref, bf_ref, q_ref, kt_ref, vt_ref, lft_ref):
    xn = _rms(h_ref[...], g_ref[...]).astype(BF16)
    q_ref[...] = (_dot(xn, wq_ref[...]) * ATTN_SCALE + sel_ref[...]).astype(BF16)
    t = _dot_nt(wt_ref[...], xn)
    kt_ref[...] = t[0:D_KV]
    vt_ref[...] = t[D_KV:2 * D_KV]
    lft_ref[...] = _log_sigmoid(t[2 * D_KV:2 * D_KV + N_HEADS] + bf_ref[...])


def _fox_proj(h, g, wq_pad, sel, w_t, bf_col):
    m = h.shape[0]
    row = lambda n: pl.BlockSpec((ROW_TILE, n), lambda i: (i, 0))
    col = lambda n: pl.BlockSpec((n, ROW_TILE), lambda i: (0, i))
    return pl.pallas_call(
        _fox_proj_kernel, grid=(m // ROW_TILE,),
        in_specs=[row(D_MODEL), _resident((1, D_MODEL)), _resident(wq_pad.shape), _resident(sel.shape),
                  _resident(w_t.shape), _resident(bf_col.shape)],
        out_specs=[row(N_HEADS * LANES), col(D_KV), col(D_KV), col(N_HEADS)],
        out_shape=[jax.ShapeDtypeStruct((m, N_HEADS * LANES), BF16),
                   jax.ShapeDtypeStruct((D_KV, m), F32), jax.ShapeDtypeStruct((D_KV, m), F32),
                   jax.ShapeDtypeStruct((N_HEADS, m), F32)],
        compiler_params=_params(("parallel",)), name="fox_proj")(h, g, wq_pad, sel, w_t, bf_col)


def _fox_query_weights(wq):
    d = wq.shape[0]
    w = jnp.pad(wq.reshape(d, N_HEADS, HEAD_DIM), ((0, 0), (0, 0), (0, LANES - HEAD_DIM)))
    lane = jnp.arange(LANES)[None, :]
    g = (jnp.arange(N_HEADS) % GROUP)[:, None]
    first = FOX_BIAS_OFFSET + 3 * g
    sel = jnp.where((lane >= first) & (lane < first + 3), -1.0, 0.0).astype(F32)
    return w.reshape(d, N_HEADS * LANES).astype(BF16), sel.reshape(1, N_HEADS * LANES)


def _cumsum_kernel(lf_ref, out_ref, carry):
    @pl.when(pl.program_id(1) == 0)
    def _():
        carry[...] = jnp.zeros_like(carry)

    tri = _upper_tri(CUMSUM_BLOCK)
    c = carry[...]
    for part in _split3(lf_ref[...]):
        c = c + _dot(part, tri)
    out_ref[...] = jnp.concatenate(_split3(c), axis=0)
    carry[...] = c[:, CUMSUM_BLOCK - 1:CUMSUM_BLOCK]


def _cumsum_parts(lf_t, batch, t):
    nblk = t // CUMSUM_BLOCK
    return pl.pallas_call(
        _cumsum_kernel, grid=(batch, nblk),
        in_specs=[pl.BlockSpec((N_HEADS, CUMSUM_BLOCK), lambda b, c: (0, b * nblk + c))],
        out_specs=pl.BlockSpec((3 * N_HEADS, CUMSUM_BLOCK), lambda b, c: (0, b * nblk + c)),
        out_shape=jax.ShapeDtypeStruct((3 * N_HEADS, batch * t), BF16),
        scratch_shapes=[pltpu.VMEM((N_HEADS, 1), F32)],
        compiler_params=_params(("parallel", "arbitrary")), name="fox_cumsum")(lf_t)


def _fox_prompt_kernel(q_ref, k_ref, v_ref, o_ref, m_s, acc_s):
    tq, tk = FOX_Q_BLOCK, FOX_K_BLOCK
    rows = GROUP * tq
    i = pl.program_id(1)
    last = (i * tq) // tk
    q_local = lax.broadcasted_iota(jnp.int32, (rows, tk), 0) % tq
    key_local = lax.broadcasted_iota(jnp.int32, (rows, tk), 1)
    causal = key_local - q_local <= i * tq - last * tk

    for pair in range(N_KV // FOX_KV_PER_LOOP):
        kvs = [pair * FOX_KV_PER_LOOP + x for x in range(FOX_KV_PER_LOOP)]
        q_st = [jnp.concatenate(
            [q_ref[:, (kv * GROUP + g) * LANES:(kv * GROUP + g + 1) * LANES] for g in range(GROUP)], axis=0)
            for kv in kvs]
        m_s[...] = jnp.full_like(m_s, -jnp.inf)
        acc_s[...] = jnp.zeros_like(acc_s)

        def step(j, masked):
            start = pl.multiple_of(j * tk, tk)
            keys = pl.ds(start, tk)
            chunks = [(x, kv, slice(c * FOX_ROW_CHUNK, (c + 1) * FOX_ROW_CHUNK))
                      for x, kv in enumerate(kvs) for c in range(rows // FOX_ROW_CHUNK)]

            def scores(x, kv, rs):
                return _dot(q_st[x][rs], k_ref[kv * LANES:(kv + 1) * LANES, keys])

            s_next = scores(*chunks[0])
            for n, (x, kv, rs) in enumerate(chunks):
                s = s_next
                if n + 1 < len(chunks):
                    s_next = scores(*chunks[n + 1])
                if masked:
                    s = jnp.where(causal[rs], s, -jnp.inf)
                m_old = m_s[x, rs]
                m_new = jnp.maximum(m_old, jnp.max(s, axis=-1, keepdims=True))
                p = jnp.exp(s - m_new).astype(BF16)
                m_s[x, rs] = m_new
                acc_s[x, rs] = (jnp.exp(m_old - m_new) * acc_s[x, rs]
                                + _dot_nt(p, v_ref[kv * LANES:(kv + 1) * LANES, keys]))

        def full_step(j, carry):
            step(j, False)
            return carry

        lax.fori_loop(0, last, full_step, 0)
        step(last, True)
        for x, kv in enumerate(kvs):
            acc = acc_s[x]
            out = acc[:, :HEAD_DIM] / acc[:, FOX_ONES_ROW:FOX_ONES_ROW + 1]
            for g in range(GROUP):
                head = kv * GROUP + g
                o_ref[:, head * HEAD_DIM:(head + 1) * HEAD_DIM] = out[g * tq:(g + 1) * tq].astype(BF16)


def _fox_prompt(q_pad, kt_pad, vt_pad, batch, t):
    nq = t // FOX_Q_BLOCK
    seq = pl.BlockSpec((N_KV * LANES, t), lambda b, i: (0, b), pipeline_mode=pl.Buffered(1))
    return pl.pallas_call(
        _fox_prompt_kernel, grid=(batch, nq),
        in_specs=[pl.BlockSpec((FOX_Q_BLOCK, N_HEADS * LANES), lambda b, i: (b * nq + i, 0)), seq, seq],
        out_specs=pl.BlockSpec((FOX_Q_BLOCK, D_MODEL), lambda b, i: (b * nq + i, 0)),
        out_shape=jax.ShapeDtypeStruct((batch * t, D_MODEL), BF16),
        scratch_shapes=[pltpu.VMEM((FOX_KV_PER_LOOP, GROUP * FOX_Q_BLOCK, 1), F32),
                        pltpu.VMEM((FOX_KV_PER_LOOP, GROUP * FOX_Q_BLOCK, LANES), F32)],
        compiler_params=_params(("parallel", "arbitrary")), name="fox_prompt")(q_pad, kt_pad, vt_pad)


def _fox_padded_keys_values(k_t, v_t, c_parts, m_p):
    kt = k_t[:, :m_p].astype(BF16).reshape(N_KV, HEAD_DIM, m_p)
    vt = v_t[:, :m_p].astype(BF16).reshape(N_KV, HEAD_DIM, m_p)
    cp = c_parts.reshape(3, N_KV, GROUP, m_p).transpose(1, 2, 0, 3).reshape(N_KV, 3 * GROUP, m_p)
    k_fill = jnp.zeros((N_KV, LANES - HEAD_DIM - 3 * GROUP, m_p), BF16)
    v_fill = jnp.zeros((N_KV, LANES - HEAD_DIM - 1, m_p), BF16)
    ones = jnp.ones((N_KV, 1, m_p), BF16)
    kt_pad = jnp.concatenate([kt, cp, k_fill], axis=1).reshape(N_KV * LANES, m_p)
    vt_pad = jnp.concatenate([vt, ones, v_fill], axis=1).reshape(N_KV * LANES, m_p)
    return kt_pad, vt_pad


def _lane_cumsum(lf, tri, carry, width, n, repeat):
    heads = lf.shape[0]
    parts = _split3(lf)
    stacked = jnp.concatenate([p[:, i * width:(i + 1) * width] for i in range(n) for p in parts], axis=0)
    cs = _dot(stacked, tri)
    key_chunk = lax.broadcasted_iota(jnp.int32, (n * width, LANES), 0) // width
    col = lax.broadcasted_iota(jnp.int32, (n * width, LANES), 1)
    before = (key_chunk < col).astype(BF16)
    offsets = carry + (_dot(parts[0], before) + _dot(parts[1], before) + _dot(parts[2], before))
    out = []
    for i in range(n):
        base = 3 * heads * i
        g = cs[base:base + heads] + cs[base + heads:base + 2 * heads] + cs[base + 2 * heads:base + 3 * heads]
        g = g + offsets[:, i:i + 1]
        out.append(jnp.concatenate([g] * repeat, axis=0))
    return out, offsets[:, n:n + 1]


def _softmax_step(s, v_t, m_s, l_s, acc_s):
    m_old = m_s[...]
    m_new = jnp.maximum(m_old, jnp.max(s, axis=-1, keepdims=True))
    p = jnp.exp(s - m_new)
    alpha = jnp.exp(m_old - m_new)
    l_s[...] = alpha * l_s[...] + jnp.sum(p, axis=-1, keepdims=True)
    m_s[...] = m_new
    acc_s[...] = alpha * acc_s[...] + _dot_nt(p.astype(BF16), v_t)


def _fox_sample_kernel(pt_ref, q_ref, kn_ref, vn_ref, lfn_ref, *rest, n_new):
    del pt_ref
    pp = PAGES_PER_STEP
    k_refs, v_refs, lf_refs = rest[:pp], rest[pp:2 * pp], rest[2 * pp:3 * pp]
    o_ref, m_s, l_s, acc_s, g_s, kcat, vcat = rest[3 * pp:]
    step = pl.program_id(1)
    rows = N_HEADS * n_new

    @pl.when(step == 0)
    def _():
        m_s[...] = jnp.full_like(m_s, -jnp.inf)
        l_s[...] = jnp.zeros_like(l_s)
        acc_s[...] = jnp.zeros_like(acc_s)
        g_s[...] = jnp.zeros_like(g_s)

    q = q_ref[...]
    for i in range(pp):
        kcat[:, i * PAGE:(i + 1) * PAGE] = k_refs[i][...].astype(BF16)
        vcat[:, i * PAGE:(i + 1) * PAGE] = v_refs[i][...].astype(BF16)
    lf = jnp.concatenate([lf_refs[i][...] for i in range(pp)], axis=1)
    gs, carry = _lane_cumsum(lf, _upper_tri(PAGE), g_s[...], PAGE, pp, n_new)
    g_s[...] = carry
    s = _dot(q, kcat[...]) - jnp.concatenate(gs, axis=1)
    _softmax_step(s, vcat[...], m_s, l_s, acc_s)

    @pl.when(step == pl.num_programs(1) - 1)
    def _():
        gn, _ = _lane_cumsum(lfn_ref[...], _upper_tri(NEW_PAD), carry, NEW_PAD, 1, n_new)
        s_new = _dot(q, kn_ref[...].astype(BF16)) - gn[0]
        u = lax.broadcasted_iota(jnp.int32, (rows, NEW_PAD), 1)
        t = lax.broadcasted_iota(jnp.int32, (rows, NEW_PAD), 0) // N_HEADS
        s_new = jnp.where(u <= t, s_new, -jnp.inf)
        _softmax_step(s_new, vn_ref[...].astype(BF16), m_s, l_s, acc_s)
        o_ref[...] = acc_s[...] / l_s[...]


def _fox_sample(q_bd, kn_t, vn_t, lfn_t, cache_kt, cache_vt, cache_lft, page_ids, n_new):
    nseq, npages = page_ids.shape
    pp = PAGES_PER_STEP
    rows = N_HEADS * n_new
    per_seq = lambda *shape: pl.BlockSpec((None,) + shape, lambda b, s, pt: (b, 0, 0))

    def page(height, i):
        return pl.BlockSpec((None, height, PAGE), lambda b, s, pt: (pt[b, s * pp + i], 0, 0))

    in_specs = ([per_seq(rows, D_KV), per_seq(D_KV, NEW_PAD), per_seq(D_KV, NEW_PAD), per_seq(N_HEADS, NEW_PAD)]
                + [page(D_KV, i) for i in range(pp)] + [page(D_KV, i) for i in range(pp)]
                + [page(N_HEADS, i) for i in range(pp)])
    grid_spec = pltpu.PrefetchScalarGridSpec(
        num_scalar_prefetch=1, grid=(nseq, npages // pp), in_specs=in_specs,
        out_specs=per_seq(rows, D_KV),
        scratch_shapes=[pltpu.VMEM((rows, 1), F32), pltpu.VMEM((rows, 1), F32), pltpu.VMEM((rows, D_KV), F32),
                        pltpu.VMEM((N_HEADS, 1), F32), pltpu.VMEM((D_KV, pp * PAGE), BF16),
                        pltpu.VMEM((D_KV, pp * PAGE), BF16)])
    return pl.pallas_call(
        functools.partial(_fox_sample_kernel, n_new=n_new), grid_spec=grid_spec,
        out_shape=jax.ShapeDtypeStruct((nseq, rows, D_KV), F32),
        compiler_params=_params(("parallel", "arbitrary")), name="fox_sample")(
            page_ids, q_bd, kn_t, vn_t, lfn_t, *([cache_kt] * pp), *([cache_vt] * pp), *([cache_lft] * pp))


def _fox_sample_queries(q_pad, nseq, t):
    q = q_pad.reshape(nseq, t, N_HEADS, LANES)[..., :HEAD_DIM]
    own = (jnp.arange(N_HEADS)[:, None] // GROUP == jnp.arange(N_KV)[None, :]).astype(q.dtype)
    return (q[:, :, :, None, :] * own[None, None, :, :, None]).reshape(nseq, t * N_HEADS, D_KV)


def _fox_sample_outputs(o, nseq, t):
    o6 = o.reshape(nseq, t, N_KV, GROUP, N_KV, HEAD_DIM)
    sel = jnp.stack([o6[:, :, kv, :, kv, :] for kv in range(N_KV)], axis=2)
    return sel.reshape(nseq * t, D_MODEL)


def _new_token_columns(x_t, m_p, nseq, t):
    n = x_t.shape[0]
    x = x_t[:, m_p:].reshape(n, nseq, t).transpose(1, 0, 2)
    return jnp.pad(x, ((0, 0), (0, 0), (0, NEW_PAD - t)))


def _swa_prompt_kernel(q_ref, kp_ref, kc_ref, vp_ref, vc_ref, sink_ref, o_ref):
    w = WINDOW
    i = pl.program_id(1)
    row = lax.broadcasted_iota(jnp.int32, (w, 2 * w), 0)
    col = lax.broadcasted_iota(jnp.int32, (w, 2 * w), 1)
    valid = (col > row) & (col <= row + w) & ((col >= w) | (i > 0))
    ones = jnp.ones((2 * w, LANES), BF16)

    def scores(kv):
        lanes = slice(kv * HEAD_DIM, (kv + 1) * HEAD_DIM)
        q_st = jnp.concatenate(
            [q_ref[:, (kv * GROUP + g) * HEAD_DIM:(kv * GROUP + g + 1) * HEAD_DIM] for g in range(GROUP)],
            axis=0)
        kk = jnp.concatenate([kp_ref[:, lanes], kc_ref[:, lanes]], axis=0)
        return _dot_nt(q_st, kk)

    s_next = scores(0)
    for kv in range(N_KV):
        lanes = slice(kv * HEAD_DIM, (kv + 1) * HEAD_DIM)
        s = s_next
        if kv + 1 < N_KV:
            s_next = scores(kv + 1)
        vv = jnp.concatenate([vp_ref[:, lanes], vc_ref[:, lanes]], axis=0)
        ps, sink_terms = [], []
        for g in range(GROUP):
            head = kv * GROUP + g
            sg = jnp.where(valid, s[g * w:(g + 1) * w], -jnp.inf)
            sink = sink_ref[0:1, head:head + 1]
            m = jnp.maximum(jnp.max(sg, axis=-1, keepdims=True), sink)
            ps.append(jnp.exp(sg - m).astype(BF16))
            sink_terms.append(jnp.exp(sink - m))
        p = jnp.concatenate(ps, axis=0)
        den = _dot(p, ones)[:, :HEAD_DIM] + jnp.concatenate(sink_terms, axis=0)
        out = _dot(p, vv) / den
        for g in range(GROUP):
            head = kv * GROUP + g
            o_ref[:, head * HEAD_DIM:(head + 1) * HEAD_DIM] = out[g * w:(g + 1) * w].astype(BF16)


def _swa_prompt(q, kb, vb, sink, batch, t):
    nb = t // WINDOW
    cur = lambda n: pl.BlockSpec((WINDOW, n), lambda b, i: (b * nb + i, 0))
    prev = lambda n: pl.BlockSpec((WINDOW, n), lambda b, i: (b * nb + jnp.maximum(i - 1, 0), 0))
    return pl.pallas_call(
        _swa_prompt_kernel, grid=(batch, nb),
        in_specs=[cur(D_MODEL), prev(D_KV), cur(D_KV), prev(D_KV), cur(D_KV), _resident((1, N_HEADS))],
        out_specs=cur(D_MODEL),
        out_shape=jax.ShapeDtypeStruct((batch * t, D_MODEL), BF16),
        compiler_params=_params(("parallel", "arbitrary")), name="swa_prompt")(q, kb, kb, vb, vb, sink)


def _swa_sample_kernel(q_ref, ck_ref, cv_ref, kn_ref, vn_ref, sink_ref, o_ref, kbuf_ref, vbuf_ref, *, n_new):
    rows = N_HEADS * n_new
    key = lax.broadcasted_iota(jnp.int32, (rows, WINDOW), 1)
    t_c = lax.broadcasted_iota(jnp.int32, (rows, WINDOW), 0) % n_new
    u = lax.broadcasted_iota(jnp.int32, (rows, NEW_PAD), 1)
    t_n = lax.broadcasted_iota(jnp.int32, (rows, NEW_PAD), 0) % n_new
    sink = sink_ref[...]
    for b in range(SWA_SEQS_PER_STEP):
        q = q_ref[b]
        ck, cv, kn, vn = ck_ref[b], cv_ref[b], kn_ref[b], vn_ref[b]
        s_c = jnp.where(key > t_c, _dot_nt(q, ck.astype(BF16)), -jnp.inf)
        s_n = jnp.where(u <= t_n, _dot_nt(q, kn.astype(BF16)), -jnp.inf)
        m = jnp.maximum(jnp.maximum(jnp.max(s_c, axis=-1, keepdims=True),
                                    jnp.max(s_n, axis=-1, keepdims=True)), sink)
        p_c = jnp.exp(s_c - m)
        p_n = jnp.exp(s_n - m)
        den = (jnp.sum(p_c, axis=-1, keepdims=True) + jnp.sum(p_n, axis=-1, keepdims=True)
               + jnp.exp(sink - m))
        o = _dot(p_c.astype(BF16), cv.astype(BF16)) + _dot(p_n.astype(BF16), vn.astype(BF16))
        o_ref[b] = o / den
        kbuf_ref[b, 0:WINDOW - n_new, :] = ck[n_new:, :]
        kbuf_ref[b, WINDOW - n_new:WINDOW, :] = kn[0:n_new, :]
        vbuf_ref[b, 0:WINDOW - n_new, :] = cv[n_new:, :]
        vbuf_ref[b, WINDOW - n_new:WINDOW, :] = vn[0:n_new, :]


def _swa_sample(q_bd, cache_k, cache_v, k_new, v_new, sink_rows, n_new):
    nseq = q_bd.shape[0]
    rows = N_HEADS * n_new
    sb = SWA_SEQS_PER_STEP
    blk = lambda a, b: pl.BlockSpec((sb, a, b), lambda i: (i, 0, 0))
    return pl.pallas_call(
        functools.partial(_swa_sample_kernel, n_new=n_new), grid=(nseq // sb,),
        in_specs=[blk(rows, D_KV), blk(WINDOW, D_KV), blk(WINDOW, D_KV), blk(NEW_PAD, D_KV), blk(NEW_PAD, D_KV),
                  _resident((rows, 1))],
        out_specs=[blk(rows, D_KV), blk(WINDOW, D_KV), blk(WINDOW, D_KV)],
        out_shape=[jax.ShapeDtypeStruct((nseq, rows, D_KV), F32),
                   jax.ShapeDtypeStruct((nseq, WINDOW, D_KV), F32),
                   jax.ShapeDtypeStruct((nseq, WINDOW, D_KV), F32)],
        compiler_params=_params(("parallel",)), name="swa_sample")(
            q_bd, cache_k, cache_v, k_new, v_new, sink_rows)


def _ret_proj_kernel(h_ref, g_ref, w_ref, cos_ref, sin_ref, q_ref, k_ref, v_ref, gate_ref):
    xn = _rms(h_ref[...], g_ref[...]).astype(BF16)
    cos = jnp.concatenate([cos_ref[...]] * RET_HEADS, axis=1)
    sin = jnp.concatenate([sin_ref[...]] * RET_HEADS, axis=1)
    d = D_MODEL
    q_ref[...] = _dot(xn, w_ref[:, 0:d]) * cos + _dot(xn, w_ref[:, d:2 * d]) * sin
    k_scale = RET_DK ** -0.5
    k_ref[...] = (_dot(xn, w_ref[:, 2 * d:3 * d]) * k_scale) * cos + (_dot(xn, w_ref[:, 3 * d:4 * d]) * k_scale) * sin
    v_ref[...] = _dot(xn, w_ref[:, 4 * d:6 * d])
    z = _dot(xn, w_ref[:, 6 * d:8 * d])
    gate_ref[...] = z / (1.0 + jnp.exp(-z))


def _ret_proj(h, g, w, cos_tab, sin_tab, prompt_tiles, tiles_per_seq):
    m = h.shape[0]
    row = lambda n: pl.BlockSpec((ROW_TILE, n), lambda i: (i, 0))
    tab = pl.BlockSpec((ROW_TILE, RET_DK),
                       lambda i: (jnp.where(i < prompt_tiles, i % tiles_per_seq, tiles_per_seq), 0))
    dv = RET_HEADS * RET_DV
    return pl.pallas_call(
        _ret_proj_kernel, grid=(m // ROW_TILE,),
        in_specs=[row(D_MODEL), _resident((1, D_MODEL)), _resident(w.shape), tab, tab],
        out_specs=[row(D_MODEL), row(D_MODEL), row(dv), row(dv)],
        out_shape=[jax.ShapeDtypeStruct((m, D_MODEL), F32), jax.ShapeDtypeStruct((m, D_MODEL), F32),
                   jax.ShapeDtypeStruct((m, dv), F32), jax.ShapeDtypeStruct((m, dv), F32)],
        compiler_params=_params(("parallel",)), name="ret_proj")(h, g, w, cos_tab, sin_tab)


def _ret_chunk_kernel(*refs, has_init):
    if has_init:
        q_ref, k_ref, v_ref, gate_ref, gn_ref, decay_ref, xi_ref, zeta_ref, gl_ref, r0_ref, y_ref, r_ref, r_s = refs
    else:
        q_ref, k_ref, v_ref, gate_ref, gn_ref, decay_ref, xi_ref, zeta_ref, gl_ref, y_ref, r_ref, r_s = refs

    @pl.when(pl.program_id(1) == 0)
    def _():
        r_s[...] = r0_ref[...] if has_init else jnp.zeros_like(r_s)

    eye = _eye(RET_DK, BF16)
    for hd in range(RET_HEADS):
        qk_cols = slice(hd * RET_DK, (hd + 1) * RET_DK)
        v_cols = slice(hd * RET_DV, (hd + 1) * RET_DV)
        q = q_ref[:, qk_cols]
        k = k_ref[:, qk_cols]
        vb = v_ref[:, v_cols].astype(BF16)
        r = r_s[hd]
        s = _dot_nt(q.astype(BF16), k.astype(BF16)) * decay_ref[hd]
        o = _dot(s.astype(BF16), vb) + _dot((q * xi_ref[hd]).astype(BF16), r.astype(BF16))
        kz = (k * zeta_ref[hd]).astype(BF16)
        kz_t = _dot_nt(eye, kz).astype(BF16)
        r_new = r * gl_ref[hd] + _dot(kz_t, vb)
        r_s[hd] = r_new
        r_ref[hd] = r_new

        mu = jnp.mean(o, axis=-1, keepdims=True)
        oc = o - mu
        var = jnp.mean(oc * oc, axis=-1, keepdims=True)
        y = oc * lax.rsqrt(var + EPS) * gn_ref[:, v_cols]
        y_ref[:, v_cols] = (gate_ref[:, v_cols] * y).astype(BF16)


def _ret_chunks(q, k, v, gate, gn, tables, batch, nchunk, chunk, r0=None):
    decay, xi, zeta, gl = tables
    has_init = r0 is not None
    dk, dv = RET_HEADS * RET_DK, RET_HEADS * RET_DV
    tok = lambda n: pl.BlockSpec((chunk, n), lambda b, c: (b * nchunk + c, 0))
    state = pl.BlockSpec((None, RET_HEADS, RET_DK, RET_DV), lambda b, c: (b, 0, 0, 0))
    in_specs = [tok(dk), tok(dk), tok(dv), tok(dv), _resident(gn.shape),
                _resident(decay.shape), _resident(xi.shape), _resident(zeta.shape), _resident(gl.shape)]
    args = [q, k, v, gate, gn, decay, xi, zeta, gl]
    if has_init:
        in_specs.append(state)
        args.append(r0)
    return pl.pallas_call(
        functools.partial(_ret_chunk_kernel, has_init=has_init), grid=(batch, nchunk),
        in_specs=in_specs, out_specs=[tok(dv), state],
        out_shape=[jax.ShapeDtypeStruct((batch * nchunk * chunk, dv), BF16),
                   jax.ShapeDtypeStruct((batch, RET_HEADS, RET_DK, RET_DV), F32)],
        scratch_shapes=[pltpu.VMEM((RET_HEADS, RET_DK, RET_DV), F32)],
        compiler_params=_params(("parallel", "arbitrary")), name="ret_chunks")(*args)


def _ret_tables(chunk, n_valid):
    lg = jnp.log(1.0 - 2.0 ** (-5.0 - jnp.arange(RET_HEADS, dtype=F32)))
    idx = jnp.arange(chunk, dtype=F32)
    rel = idx[:, None] - idx[None, :]
    decay = jnp.where(rel >= 0, jnp.exp(lg[:, None, None] * jnp.maximum(rel, 0.0)), 0.0)
    xi = jnp.exp(lg[:, None] * (idx[None, :] + 1.0))[:, :, None]
    zeta = jnp.where(idx[None, :] < n_valid, jnp.exp(lg[:, None] * (n_valid - 1.0 - idx[None, :])), 0.0)[:, :, None]
    gl = jnp.exp(lg * n_valid)[:, None, None]
    return decay, xi, zeta, gl


def _rotation_tables(pos):
    freqs = 1.0 / (RET_THETA_BASE ** jnp.linspace(0.0, 1.0, RET_DK // 2, dtype=F32))
    ang = pos.astype(F32)[:, None] * freqs[None, :]
    cos = jnp.repeat(jnp.cos(ang), 2, axis=1)
    sign = jnp.tile(jnp.array([-1.0, 1.0], F32), RET_DK // 2)
    sin = jnp.repeat(jnp.sin(ang), 2, axis=1) * sign[None, :]
    return cos, sin


def _swap_pairs(w):
    d_in, d_out = w.shape
    return w.reshape(d_in, d_out // 2, 2)[:, :, ::-1].reshape(d_in, d_out)


def _post_kernel(o_ref, h_ref, p_ref, wo_ref, gm_ref, up_ref, dn_ref, gp_ref, wg_ref, wp_ref, gf_ref, out_ref,
                 *, final):
    h = h_ref[...] + _dot(o_ref[...], wo_ref[...])
    xn = _rms(h, gm_ref[...]).astype(BF16)
    mlp = jnp.zeros_like(h)
    for c in range(D_FF // D_MODEL):
        cols = slice(c * D_MODEL, (c + 1) * D_MODEL)
        u = jnp.maximum(_dot(xn, up_ref[:, cols]), 0.0)
        mlp = mlp + _dot((u * u).astype(BF16), dn_ref[cols, :])
    h = h + mlp
    xn = _rms(h, gp_ref[...]).astype(BF16)
    gate = 1.0 / (1.0 + jnp.exp(-_dot(xn, wg_ref[...])))
    h = h + gate * _dot(p_ref[...].astype(BF16), wp_ref[...])
    if final:
        h = _rms(h, gf_ref[...])
    out_ref[...] = h


def _post(o, h, p, wo, gm, up, dn, gp, wg, wp, gf, final):
    m = h.shape[0]
    row = lambda n: pl.BlockSpec((ROW_TILE, n), lambda i: (i, 0))
    vec = _resident((1, D_MODEL))
    return pl.pallas_call(
        functools.partial(_post_kernel, final=final), grid=(m // ROW_TILE,),
        in_specs=[row(o.shape[1]), row(D_MODEL), row(D_PLE), _resident(wo.shape), vec, _resident(up.shape),
                  _resident(dn.shape), vec, _resident(wg.shape), _resident(wp.shape), vec],
        out_specs=row(D_MODEL), out_shape=jax.ShapeDtypeStruct((m, D_MODEL), F32),
        compiler_params=_params(("parallel",)), name="post_mixer")(o, h, p, wo, gm, up, dn, gp, wg, wp, gf)


def _block_diag_queries(q, nseq, t):
    q5 = q.reshape(nseq, t, N_KV, GROUP, HEAD_DIM).transpose(0, 2, 3, 1, 4)
    q4 = q5.reshape(nseq, N_KV, GROUP * t, 1, HEAD_DIM)
    eye = jnp.eye(N_KV, dtype=q.dtype)[None, :, None, :, None]
    return (q4 * eye).reshape(nseq, N_KV * GROUP * t, D_KV)


def _block_diag_outputs(o, nseq, t):
    o5 = o.reshape(nseq, N_KV, GROUP * t, N_KV, HEAD_DIM)
    sel = jnp.stack([o5[:, kv, :, kv, :] for kv in range(N_KV)], axis=1)
    return sel.reshape(nseq, N_KV, GROUP, t, HEAD_DIM).transpose(0, 3, 1, 2, 4).reshape(nseq * t, D_MODEL)


def _pad_tokens(x, nseq, t, width):
    return jnp.pad(x.reshape(nseq, t, width), ((0, 0), (0, NEW_PAD - t), (0, 0)))


def kernel(x_prompt, x_sample, cache_fox_k, cache_fox_v, cache_fox_logf, cache_swa_k, cache_swa_v, state_ret, page_table, p_prompt, p_sample, fox_wq, fox_wk, fox_wv, fox_wf, fox_bf, fox_wo, swa_wq, swa_wk, swa_wv, swa_sink, swa_wo, ret_wq, ret_wk, ret_wv, ret_wg, ret_gn, ret_wo, norm_mix, norm_mlp, mlp_up, mlp_down, norm_ple, ple_gate, ple_proj, norm_final):
    batch, t_p, d = x_prompt.shape
    nseq, t_s, _ = x_sample.shape
    depth = norm_mix.shape[0]
    m_p = batch * t_p
    m_s = nseq * t_s
    n_pool = cache_fox_k.shape[1]
    past = page_table.shape[1] * PAGE
    assert d == D_MODEL and (m_p + m_s) % ROW_TILE == 0 and m_p % ROW_TILE == 0 and m_s == ROW_TILE
    assert t_p % ROW_TILE == 0 and t_s <= NEW_PAD and N_HEADS * t_s == 64

    h = jnp.concatenate([x_prompt.reshape(m_p, d), x_sample.reshape(m_s, d)], axis=0)
    p_all = jnp.concatenate([p_prompt.reshape(depth, m_p, D_PLE), p_sample.reshape(depth, m_s, D_PLE)], axis=1)
    vec = lambda a: a.reshape(1, -1)

    fox_cache_kt = cache_fox_k.transpose(0, 1, 3, 4, 2).reshape(-1, D_KV, PAGE)
    fox_cache_vt = cache_fox_v.transpose(0, 1, 3, 4, 2).reshape(-1, D_KV, PAGE)
    fox_cache_lft = cache_fox_logf.transpose(0, 1, 3, 2).reshape(-1, N_HEADS, PAGE)

    outs = {name: [] for name in ("fkp", "fvp", "flp", "fks", "fvs", "fls", "skp", "svp", "sks", "svs", "rsp", "rss")}

    for i in range(depth):
        kind, j = i % N_MIXERS, i // N_MIXERS
        g_mix = vec(norm_mix[i])
        if kind == 0:
            wq_pad, sel = _fox_query_weights(fox_wq[j])
            wf_t = jnp.pad(fox_wf[j].T, ((0, LANES - N_HEADS), (0, 0)))
            w_t = jnp.concatenate([fox_wk[j].T, fox_wv[j].T, wf_t], axis=0).astype(BF16)
            q_pad, k_t, v_t, lf_t = _fox_proj(h, g_mix, wq_pad, sel, w_t, fox_bf[j].reshape(N_HEADS, 1))
            c_parts = _cumsum_parts(lf_t, batch, t_p)
            kt_pad, vt_pad = _fox_padded_keys_values(k_t, v_t, c_parts, m_p)
            o_p = _fox_prompt(q_pad, kt_pad, vt_pad, batch, t_p)
            o_bd = _fox_sample(_fox_sample_queries(q_pad[m_p:], nseq, t_s),
                               _new_token_columns(k_t, m_p, nseq, t_s), _new_token_columns(v_t, m_p, nseq, t_s),
                               _new_token_columns(lf_t, m_p, nseq, t_s),
                               fox_cache_kt, fox_cache_vt, fox_cache_lft, page_table + j * n_pool, t_s)
            o_s = _fox_sample_outputs(o_bd, nseq, t_s).astype(BF16)
            o = jnp.concatenate([o_p, o_s], axis=0)
            wo = fox_wo[j]
            prompt_rows = lambda x_t, n: x_t[:, :m_p].reshape(n, batch, t_p).transpose(1, 2, 0)
            sample_rows = lambda x_t, n: x_t[:, m_p:].reshape(n, nseq, t_s).transpose(1, 2, 0)
            outs["fkp"].append(prompt_rows(k_t, D_KV).reshape(batch, t_p, N_KV, HEAD_DIM))
            outs["fvp"].append(prompt_rows(v_t, D_KV).reshape(batch, t_p, N_KV, HEAD_DIM))
            outs["flp"].append(prompt_rows(lf_t, N_HEADS))
            outs["fks"].append(sample_rows(k_t, D_KV).reshape(nseq, t_s, N_KV, HEAD_DIM))
            outs["fvs"].append(sample_rows(v_t, D_KV).reshape(nseq, t_s, N_KV, HEAD_DIM))
            outs["fls"].append(sample_rows(lf_t, N_HEADS))
        elif kind == 1:
            w = jnp.concatenate([swa_wq[j], swa_wk[j], swa_wv[j]], axis=1).astype(BF16)
            q, k, v, kb, vb = _swa_proj(h, g_mix, w)
            o_p = _swa_prompt(q, kb, vb, vec(swa_sink[j]), batch, t_p)
            q_bd = _block_diag_queries(q[m_p:], nseq, t_s)
            sink_rows = jnp.repeat(swa_sink[j], t_s).reshape(N_HEADS * t_s, 1)
            win = cache_swa_k.shape[2]
            o_bd, kbuf, vbuf = _swa_sample(
                q_bd, cache_swa_k[j].reshape(nseq, win, D_KV), cache_swa_v[j].reshape(nseq, win, D_KV),
                _pad_tokens(k[m_p:], nseq, t_s, D_KV), _pad_tokens(v[m_p:], nseq, t_s, D_KV), sink_rows, t_s)
            o_s = _block_diag_outputs(o_bd, nseq, t_s).astype(BF16)
            o = jnp.concatenate([o_p, o_s], axis=0)
            wo = swa_wo[j]
            wp = min(WINDOW, t_p)
            outs["skp"].append(k[:m_p].reshape(batch, t_p, N_KV, HEAD_DIM)[:, -wp:])
            outs["svp"].append(v[:m_p].reshape(batch, t_p, N_KV, HEAD_DIM)[:, -wp:])
            outs["sks"].append(kbuf.reshape(nseq, win, N_KV, HEAD_DIM))
            outs["svs"].append(vbuf.reshape(nseq, win, N_KV, HEAD_DIM))
        else:
            w = jnp.concatenate([ret_wq[j], _swap_pairs(ret_wq[j]), ret_wk[j], _swap_pairs(ret_wk[j]),
                                 ret_wv[j], ret_wg[j]], axis=1).astype(BF16)
            cos_p, sin_p = _rotation_tables(jnp.arange(t_p))
            cos_s, sin_s = _rotation_tables(past + jnp.arange(t_s))
            cos_tab = jnp.concatenate([cos_p, jnp.tile(cos_s, (nseq, 1))], axis=0)
            sin_tab = jnp.concatenate([sin_p, jnp.tile(sin_s, (nseq, 1))], axis=0)
            qr, kr, v, gate = _ret_proj(h, g_mix, w, cos_tab, sin_tab, m_p // ROW_TILE, t_p // ROW_TILE)
            gn = vec(ret_gn[j])
            y_p, r_p = _ret_chunks(qr, kr, v, gate, gn, _ret_tables(RET_CHUNK, RET_CHUNK),
                                   batch, t_p // RET_CHUNK, RET_CHUNK)
            dv = RET_HEADS * RET_DV
            pad = lambda a, n: _pad_tokens(a[m_p:], nseq, t_s, n).reshape(nseq * NEW_PAD, n)
            y_s, r_s = _ret_chunks(pad(qr, D_MODEL), pad(kr, D_MODEL), pad(v, dv), pad(gate, dv), gn,
                                   _ret_tables(NEW_PAD, t_s), nseq, 1, NEW_PAD,
                                   r0=state_ret[j].astype(F32))
            y_s = y_s.reshape(nseq, NEW_PAD, dv)[:, :t_s].reshape(m_s, dv)
            o = jnp.concatenate([y_p, y_s], axis=0)
            wo = ret_wo[j]
            outs["rsp"].append(r_p)
            outs["rss"].append(r_s)
        h = _post(o, h, p_all[i], wo.astype(BF16), vec(norm_mlp[i]), mlp_up[i].astype(BF16),
                  mlp_down[i].astype(BF16), vec(norm_ple[i]), ple_gate[i].astype(BF16),
                  ple_proj[i].astype(BF16), vec(norm_final), final=(i == depth - 1))

    y_prompt = h[:m_p].reshape(batch, t_p, d)
    y_sample = h[m_p:].reshape(nseq, t_s, d)
    st = lambda name: jnp.stack(outs[name])
    return (y_prompt, y_sample, st("fkp"), st("fvp"), st("flp"), st("fks"), st("fvs"), st("fls"),
            st("skp"), st("svp"), st("sks"), st("svs"), st("rsp"), st("rss"))
```
